```python
import math
import jax, jax.numpy as jnp
from jax import lax
import numpy as np

D_MODEL = 2048
BATCH = 1
SEQ = 16384
DEPTH = 4

GRID_W = 64
CTX_LEN = 256
N_MIXERS = 4
MIX_FOURIER, MIX_CONV, MIX_ATTN, MIX_SSM = 0, 1, 2, 3
FOURIER_GROUPS = 8
FOURIER_GROUP_DIM = D_MODEL // FOURIER_GROUPS
CONV_WIDTH = 3
HEAD_DIM = 128
N_HEADS = D_MODEL // HEAD_DIM
N_KV_HEADS = 4
Q_PER_KV = N_HEADS // N_KV_HEADS
ROPE_AXIS_PAIRS = HEAD_DIM // 4
ROPE_THETA = 10000.0
Q_BLOCK = 128
SSM_GROUP_DIM = 16
SSM_GROUPS = D_MODEL // SSM_GROUP_DIM
SSM_STATE = 64
SSM_CHUNK = 128
DT_MIN = 1e-3
DT_MAX = 1e-1
N_EXPERTS = 32
TOP_K = 4
D_EXPERT = 1024
SWIGLU_LIMIT = 7.0
SWIGLU_ALPHA = 1.702
MOE_BLOCK = 256
NORM_EPS = 1e-6

kernel_name = 'hybrid_interleaved_diffusion_moe_block'


def _mixer_layers(m):
    return len(range(m, DEPTH, N_MIXERS))


def _rmsnorm(x, g):
    xf = x.astype(jnp.float32)
    y = xf * lax.rsqrt(jnp.mean(xf * xf, axis=-1, keepdims=True) + NORM_EPS)
    return (y * g.astype(jnp.float32)).astype(x.dtype)


def _modulate(x, g, shift, scale):
    return _rmsnorm(x, g) * (1 + scale) + shift


def _split_mod(cond, w, b):
    return jnp.split(jax.nn.silu(cond) @ w + b, 6, axis=-1)


def _fourier_mix(h, hc, w_out, b_out):
    def mix(u):
        bsz, n, _ = u.shape
        ug = u.astype(jnp.float32).reshape(bsz, n, FOURIER_GROUPS, FOURIER_GROUP_DIM)
        f = jnp.fft.fft2(ug, axes=(1, 3), norm='ortho').real
        return f.reshape(bsz, n, D_MODEL).astype(u.dtype) @ w_out + b_out
    return mix(h), (None if hc is None else mix(hc))


def _short_conv_mix(h, hc, w_in, conv_k, conv_b, w_out):
    def mix(u):
        gate_b, gate_c, v = jnp.split(u @ w_in, 3, axis=-1)
        z = lax.conv_general_dilated(
            gate_c * v, conv_k[:, None, :], window_strides=(1,),
            padding=((CONV_WIDTH // 2, CONV_WIDTH // 2),),
            dimension_numbers=('NWC', 'WIO', 'NWC'), feature_group_count=D_MODEL) + conv_b
        return (gate_b * z) @ w_out
    return mix(h), (None if hc is None else mix(hc))


def _axial_rope_tables(n):
    rows = n // GRID_W
    r = jnp.repeat(jnp.arange(rows, dtype=jnp.float32), GRID_W)
    col = jnp.tile(jnp.arange(GRID_W, dtype=jnp.float32), rows)
    inv = ROPE_THETA ** (-jnp.arange(ROPE_AXIS_PAIRS, dtype=jnp.float32) / ROPE_AXIS_PAIRS)
    ang = jnp.concatenate([r[:, None] * inv, col[:, None] * inv], axis=-1)
    return jnp.cos(ang), jnp.sin(ang)


def _apply_rope(x, cos, sin):
    xf = x.astype(jnp.float32)
    x1, x2 = xf[..., :HEAD_DIM // 2], xf[..., HEAD_DIM // 2:]
    cs, sn = cos[None, :, None, :], sin[None, :, None, :]
    return jnp.concatenate([x1 * cs - x2 * sn, x2 * cs + x1 * sn], axis=-1).astype(x.dtype)


def _attend(q, k, v):
    s = jnp.einsum('bqngd,bsnd->bngqs', q, k).astype(jnp.float32) * HEAD_DIM ** -0.5
    p = jax.nn.softmax(s, axis=-1).astype(v.dtype)
    return jnp.einsum('bngqs,bsnd->bqngd', p, v)


def _gqa_mix(h, hc, wq, wk, wv, wo, q_g, k_g, ctx_out):
    bsz, n, _ = h.shape

    def proj_q(u):
        return _rmsnorm((u @ wq).reshape(u.shape[0], u.shape[1], N_HEADS, HEAD_DIM), q_g)

    def proj_kv(u):
        k = _rmsnorm((u @ wk).reshape(u.shape[0], u.shape[1], N_KV_HEADS, HEAD_DIM), k_g)
        v = (u @ wv).reshape(u.shape[0], u.shape[1], N_KV_HEADS, HEAD_DIM)
        return k, v

    cos, sin = _axial_rope_tables(n)
    q = _apply_rope(proj_q(h), cos, sin).reshape(bsz, n // Q_BLOCK, Q_BLOCK, N_KV_HEADS, Q_PER_KV, HEAD_DIM)
    k, v = proj_kv(h)
    k = _apply_rope(k, cos, sin)
    kc, vc = proj_kv(hc)
    k_all = jnp.concatenate([k, kc], axis=1)
    v_all = jnp.concatenate([v, vc], axis=1)
    o = lax.map(lambda qb: _attend(qb, k_all, v_all), jnp.moveaxis(q, 1, 0))
    y = jnp.moveaxis(o, 0, 1).reshape(bsz, n, N_HEADS * HEAD_DIM) @ wo
    if not ctx_out:
        return y, None
    m = hc.shape[1]
    qc = proj_q(hc).reshape(bsz, m, N_KV_HEADS, Q_PER_KV, HEAD_DIM)
    yc = _attend(qc, kc, vc).reshape(bsz, m, N_HEADS * HEAD_DIM) @ wo
    return y, yc


def _s5_discretise(a_re, a_im, log_dt, b_re, b_im, c_re, c_im):
    f = jnp.float32
    a = lax.complex(a_re.astype(f), a_im.astype(f))
    a_bar = jnp.exp(a * jnp.exp(log_dt.astype(f))[:, None])
    b_bar = ((a_bar - 1) / a)[..., None] * lax.complex(b_re.astype(f), b_im.astype(f))
    c_mat = lax.complex(c_re.astype(f), c_im.astype(f))
    return a_bar, b_bar, c_mat


def _linear_recurrence(e1, e2):
    a1, b1 = e1
    a2, b2 = e2
    return a1 * a2, a2 * b1 + b2


def _s5_scan(u, h0, a_bar, b_bar, c_mat, with_y):
    bsz, n, _ = u.shape
    uc = u.astype(jnp.float32).reshape(bsz, n // SSM_CHUNK, SSM_CHUNK, SSM_GROUPS, SSM_GROUP_DIM)

    def step(h, u_blk):
        bu = jnp.einsum('blgc,gpc->blgp', u_blk.astype(jnp.complex64), b_bar)
        bu = bu.at[:, 0].add(a_bar * h)
        _, hs = lax.associative_scan(_linear_recurrence, (jnp.broadcast_to(a_bar, bu.shape), bu), axis=1)
        y = jnp.einsum('blgp,gcp->blgc', hs, c_mat).real if with_y else None
        return hs[:, -1], y

    h_last, ys = lax.scan(step, h0, jnp.moveaxis(uc, 1, 0))
    y = None if ys is None else jnp.moveaxis(ys, 0, 1).reshape(bsz, n, D_MODEL)
    return y, h_last


def _maybe_flip(t, rev):
    return jnp.flip(t, axis=1) if rev else t


def _s5_mix(h, hc, a_re, a_im, log_dt, b_re, b_im, c_re, c_im, d_skip, glu_w, glu_b, ctx_out):
    bsz = h.shape[0]
    d = d_skip.astype(jnp.float32)
    y = d * h.astype(jnp.float32)
    yc = d * hc.astype(jnp.float32) if ctx_out else None
    for direction in range(2):
        rev = direction == 1
        a_bar, b_bar, c_mat = _s5_discretise(a_re[direction], a_im[direction], log_dt[direction],
                                             b_re[direction], b_im[direction], c_re[direction], c_im[direction])
        h0 = jnp.zeros((bsz, SSM_GROUPS, SSM_STATE), jnp.complex64)
        yc_dir, h_ctx = _s5_scan(_maybe_flip(hc, rev), h0, a_bar, b_bar, c_mat, ctx_out)
        y_dir, _ = _s5_scan(_maybe_flip(h, rev), h_ctx, a_bar, b_bar, c_mat, True)
        y = y + _maybe_flip(y_dir, rev)
        if ctx_out:
            yc = yc + _maybe_flip(yc_dir, rev)

    def glu(t, like):
        z = jax.nn.gelu(t.astype(like.dtype))
        a, g = jnp.split(z @ glu_w + glu_b, 2, axis=-1)
        return a * jax.nn.sigmoid(g)

    return glu(y, h), (glu(yc, hc) if ctx_out else None)


def _moe(h, w_r, b_r, w1, b1, w2, b2):
    t, d = h.shape
    logits = h.astype(jnp.float32) @ w_r.astype(jnp.float32) + b_r.astype(jnp.float32)
    top_v, top_e = lax.top_k(logits, TOP_K)
    gates = jax.nn.softmax(top_v, axis=-1)
    n_assign = t * TOP_K
    flat_e = top_e.reshape(-1)
    flat_g = gates.reshape(-1)
    flat_tok = jnp.arange(n_assign, dtype=jnp.int32) // TOP_K
    order = jnp.argsort(flat_e)
    sorted_e = flat_e[order]
    counts = jnp.bincount(flat_e, length=N_EXPERTS)
    start = jnp.cumsum(counts) - counts
    padded = (counts + MOE_BLOCK - 1) // MOE_BLOCK * MOE_BLOCK
    pend = jnp.cumsum(padded)
    pstart = pend - padded
    dest = pstart[sorted_e] + jnp.arange(n_assign, dtype=jnp.int32) - start[sorted_e]
    n_rows = -(-n_assign // MOE_BLOCK) * MOE_BLOCK + N_EXPERTS * MOE_BLOCK
    n_blocks = n_rows // MOE_BLOCK
    row_tok = jnp.zeros((n_rows,), jnp.int32).at[dest].set(flat_tok[order])
    row_gate = jnp.zeros((n_rows,), jnp.float32).at[dest].set(flat_g[order])
    block_e = jnp.minimum(jnp.searchsorted(pend, jnp.arange(n_blocks, dtype=jnp.int32) * MOE_BLOCK, side='right'),
                          N_EXPERTS - 1)
    xb = h[row_tok].reshape(n_blocks, MOE_BLOCK, d)

    def expert_block(args):
        xe, e = args
        gate, lin = jnp.split(xe @ w1[e] + b1[e], 2, axis=-1)
        gate = jnp.minimum(gate, SWIGLU_LIMIT)
        lin = jnp.clip(lin, -SWIGLU_LIMIT, SWIGLU_LIMIT)
        return ((lin + 1) * gate * jax.nn.sigmoid(SWIGLU_ALPHA * gate)) @ w2[e] + b2[e]

    yb = lax.map(expert_block, (xb, block_e))
    y = jax.ops.segment_sum(yb.reshape(n_rows, d).astype(jnp.float32) * row_gate[:, None], row_tok,
                            num_segments=t)
    return y.astype(h.dtype)


def setup_inputs(seed: int = 0) -> dict:
    key = jax.random.key(seed)
    ks = iter(jax.random.split(key, 48))
    f = jnp.float32
    D = D_MODEL

    def nrm(shape, scale):
        return jax.random.normal(next(ks), shape, f) * scale

    n_a, n_b, n_c, n_d = (_mixer_layers(m) for m in (MIX_FOURIER, MIX_CONV, MIX_ATTN, MIX_SSM))
    x = nrm((BATCH, SEQ, D), 1.0)
    c = nrm((BATCH, D), 1.0)
    ctx = nrm((BATCH, CTX_LEN, D), 1.0)
    c_ctx = nrm((D,), 1.0)
    ada_w = nrm((DEPTH, D, 6 * D), 0.5 * D ** -0.5)
    ada_b = nrm((DEPTH, 6 * D), 0.01)
    norm1_g = 1.0 + nrm((DEPTH, D), 0.02)
    norm2_g = 1.0 + nrm((DEPTH, D), 0.02)
    fourier_w_out = nrm((n_a, D, D), D ** -0.5)
    fourier_b_out = nrm((n_a, D), 0.01)
    conv_w_in = nrm((n_b, D, 3 * D), D ** -0.5)
    conv_k = nrm((n_b, CONV_WIDTH, D), CONV_WIDTH ** -0.5)
    conv_b = nrm((n_b, D), 0.01)
    conv_w_out = nrm((n_b, D, D), D ** -0.5)
    attn_wq = nrm((n_c, D, N_HEADS * HEAD_DIM), D ** -0.5)
    attn_wk = nrm((n_c, D, N_KV_HEADS * HEAD_DIM), D ** -0.5)
    attn_wv = nrm((n_c, D, N_KV_HEADS * HEAD_DIM), D ** -0.5)
    attn_wo = nrm((n_c, N_HEADS * HEAD_DIM, D), (N_HEADS * HEAD_DIM) ** -0.5)
    attn_q_g = 1.0 + nrm((n_c, HEAD_DIM), 0.02)
    attn_k_g = 1.0 + nrm((n_c, HEAD_DIM), 0.02)
    ssm_shape = (n_d, 2, SSM_GROUPS, SSM_STATE)
    ssm_a_re = -0.5 + nrm(ssm_shape, 0.01)
    ssm_a_im = math.pi * jnp.arange(SSM_STATE, dtype=f) + nrm(ssm_shape, 0.01)
    ssm_log_dt = jax.random.uniform(next(ks), (n_d, 2, SSM_GROUPS), f, math.log(DT_MIN), math.log(DT_MAX))
    ssm_b_re = nrm((n_d, 2, SSM_GROUPS, SSM_STATE, SSM_GROUP_DIM), (2 * SSM_GROUP_DIM) ** -0.5)
    ssm_b_im = nrm((n_d, 2, SSM_GROUPS, SSM_STATE, SSM_GROUP_DIM), (2 * SSM_GROUP_DIM) ** -0.5)
    ssm_c_re = nrm((n_d, 2, SSM_GROUPS, SSM_GROUP_DIM, SSM_STATE), SSM_STATE ** -0.5)
    ssm_c_im = nrm((n_d, 2, SSM_GROUPS, SSM_GROUP_DIM, SSM_STATE), SSM_STATE ** -0.5)
    ssm_d = nrm((n_d, D), 1.0)
    ssm_glu_w = nrm((n_d, D, 2 * D), D ** -0.5)
    ssm_glu_b = nrm((n_d, 2 * D), 0.01)
    router_w = nrm((DEPTH, D, N_EXPERTS), D ** -0.5)
    router_b = nrm((DEPTH, N_EXPERTS), 0.01)
    moe_w1 = nrm((DEPTH, N_EXPERTS, D, 2 * D_EXPERT), D ** -0.5)
    moe_b1 = nrm((DEPTH, N_EXPERTS, 2 * D_EXPERT), 0.01)
    moe_w2 = nrm((DEPTH, N_EXPERTS, D_EXPERT, D), D_EXPERT ** -0.5)
    moe_b2 = nrm((DEPTH, N_EXPERTS, D), 0.01)
    return {'x': x, 'c': c, 'ctx': ctx, 'c_ctx': c_ctx, 'ada_w': ada_w, 'ada_b': ada_b,
            'norm1_g': norm1_g, 'norm2_g': norm2_g,
            'fourier_w_out': fourier_w_out, 'fourier_b_out': fourier_b_out,
            'conv_w_in': conv_w_in, 'conv_k': conv_k, 'conv_b': conv_b, 'conv_w_out': conv_w_out,
            'attn_wq': attn_wq, 'attn_wk': attn_wk, 'attn_wv': attn_wv, 'attn_wo': attn_wo,
            'attn_q_g': attn_q_g, 'attn_k_g': attn_k_g,
            'ssm_a_re': ssm_a_re, 'ssm_a_im': ssm_a_im, 'ssm_log_dt': ssm_log_dt,
            'ssm_b_re': ssm_b_re, 'ssm_b_im': ssm_b_im, 'ssm_c_re': ssm_c_re, 'ssm_c_im': ssm_c_im,
            'ssm_d': ssm_d, 'ssm_glu_w': ssm_glu_w, 'ssm_glu_b': ssm_glu_b,
            'router_w': router_w, 'router_b': router_b,
            'moe_w1': moe_w1, 'moe_b1': moe_b1, 'moe_w2': moe_w2, 'moe_b2': moe_b2}


def reference(x, c, ctx, c_ctx, ada_w, ada_b, norm1_g, norm2_g, fourier_w_out, fourier_b_out,
              conv_w_in, conv_k, conv_b, conv_w_out, attn_wq, attn_wk, attn_wv, attn_wo, attn_q_g, attn_k_g,
              ssm_a_re, ssm_a_im, ssm_log_dt, ssm_b_re, ssm_b_im, ssm_c_re, ssm_c_im, ssm_d, ssm_glu_w, ssm_glu_b,
              router_w, router_b, moe_w1, moe_b1, moe_w2, moe_b2):
    d = x.shape[-1]
    x_ctx = ctx
    for i in range(DEPTH):
        m, j = i % N_MIXERS, i // N_MIXERS
        last = i == DEPTH - 1
        ctx_out = not last
        sh1, sc1, g1, sh2, sc2, g2 = [t[:, None, :] for t in _split_mod(c, ada_w[i], ada_b[i])]
        csh1, csc1, cg1, csh2, csc2, cg2 = _split_mod(c_ctx, ada_w[i], ada_b[i])
        h = _modulate(x, norm1_g[i], sh1, sc1)
        needs_ctx_in = ctx_out or m in (MIX_ATTN, MIX_SSM)
        hc = _modulate(x_ctx, norm1_g[i], csh1, csc1) if needs_ctx_in else None
        if m == MIX_FOURIER:
            y, yc = _fourier_mix(h, hc, fourier_w_out[j], fourier_b_out[j])
        elif m == MIX_CONV:
            y, yc = _short_conv_mix(h, hc, conv_w_in[j], conv_k[j], conv_b[j], conv_w_out[j])
        elif m == MIX_ATTN:
            y, yc = _gqa_mix(h, hc, attn_wq[j], attn_wk[j], attn_wv[j], attn_wo[j], attn_q_g[j], attn_k_g[j],
                             ctx_out)
        else:
            y, yc = _s5_mix(h, hc, ssm_a_re[j], ssm_a_im[j], ssm_log_dt[j], ssm_b_re[j], ssm_b_im[j],
                            ssm_c_re[j], ssm_c_im[j], ssm_d[j], ssm_glu_w[j], ssm_glu_b[j], ctx_out)
        x = x + g1 * y
        h2 = _modulate(x, norm2_g[i], sh2, sc2)
        moe_args = (router_w[i], router_b[i], moe_w1[i], moe_b1[i], moe_w2[i], moe_b2[i])
        if last:
            x = x + g2 * _moe(h2.reshape(-1, d), *moe_args).reshape(h2.shape)
        else:
            x_ctx = x_ctx + cg1 * yc
            hc2 = _modulate(x_ctx, norm2_g[i], csh2, csc2)
            n_lat = h2.shape[0] * h2.shape[1]
            out = _moe(jnp.concatenate([h2.reshape(-1, d), hc2.reshape(-1, d)], axis=0), *moe_args)
            x = x + g2 * out[:n_lat].reshape(h2.shape)
            x_ctx = x_ctx + cg2 * out[n_lat:].reshape(hc2.shape)
    return x
```

```python
import functools
import math

import numpy as np
import jax
import jax.numpy as jnp
from jax import lax
from jax.experimental import pallas as pl
from jax.experimental.pallas import tpu as pltpu

F32 = jnp.float32
BF16 = jnp.bfloat16

D_MODEL = 2048
GRID_W = 64
CTX_LEN = 256
FOURIER_GROUPS = 8
FOURIER_GROUP_DIM = D_MODEL // FOURIER_GROUPS
HEAD_DIM = 128
N_HEADS = D_MODEL // HEAD_DIM
N_KV_HEADS = 4
Q_PER_KV = N_HEADS // N_KV_HEADS
ROPE_AXIS_PAIRS = HEAD_DIM // 4
ROPE_THETA = 10000.0
SSM_GROUP_DIM = 16
SSM_GROUPS = D_MODEL // SSM_GROUP_DIM
SSM_STATE = 64
N_EXPERTS = 32
TOP_K = 4
D_EXPERT = 1024
SWIGLU_LIMIT = 7.0
SWIGLU_ALPHA = 1.702
NORM_EPS = 1e-6

LANES = 128
TM = 256
DFT_N2 = 128
SSM_SUB = 8
SSM_CB = D_MODEL // LANES
SSM_GPB = LANES // SSM_GROUP_DIM
MOE_TM = 512
ATT_TK = 512
VMEM_LIMIT = 56 * 2 ** 20


def _params(sem, vmem=VMEM_LIMIT):
    return pltpu.CompilerParams(dimension_semantics=sem, vmem_limit_bytes=vmem)


def _mod_row(i):
    return jnp.minimum(i, 1)


def _modulate(x, g, shift, scale):
    ms = jnp.mean(x * x, axis=-1, keepdims=True)
    return (x * lax.rsqrt(ms + NORM_EPS) * g) * (1.0 + scale) + shift


def _dot(a, b):
    return jnp.dot(a, b, preferred_element_type=F32)


def _ada_kernel(c_ref, w_ref, b_ref, o_ref):
    c = c_ref[...]
    s = c * jax.nn.sigmoid(c)
    o_ref[0] = jnp.dot(s, w_ref[0], precision=lax.Precision.HIGHEST, preferred_element_type=F32) + b_ref[0]


def _ada_all(cond8, ada_w, ada_b):
    depth, d, n6 = ada_w.shape
    tn = 1024
    return pl.pallas_call(
        _ada_kernel,
        grid=(depth, n6 // tn),
        in_specs=[pl.BlockSpec((8, d), lambda l, j: (0, 0)),
                  pl.BlockSpec((1, d, tn), lambda l, j: (l, 0, j)),
                  pl.BlockSpec((1, 1, tn), lambda l, j: (l, 0, j))],
        out_specs=pl.BlockSpec((1, 8, tn), lambda l, j: (l, 0, j)),
        out_shape=jax.ShapeDtypeStruct((depth, 8, n6), F32),
        compiler_params=_params(("parallel", "parallel")),
        name="ada",
    )(cond8, ada_w, ada_b.reshape(depth, 1, n6))


def _prep_kernel(x_ref, g_ref, m_ref, o_ref, *, which):
    m = m_ref[0]
    h = _modulate(x_ref[...], g_ref[...], m[3 * which:3 * which + 1], m[3 * which + 1:3 * which + 2])
    o_ref[...] = h.astype(o_ref.dtype)


def _prep(xa, g, mods, which):
    t, d = xa.shape
    return pl.pallas_call(
        functools.partial(_prep_kernel, which=which),
        grid=(t // TM,),
        in_specs=[pl.BlockSpec((TM, d), lambda i: (i, 0)),
                  pl.BlockSpec((1, d), lambda i: (0, 0)),
                  pl.BlockSpec((1, 6, d), lambda i: (_mod_row(i), 0, 0))],
        out_specs=pl.BlockSpec((TM, d), lambda i: (i, 0)),
        out_shape=jax.ShapeDtypeStruct((t, d), BF16),
        compiler_params=_params(("parallel",)),
        name="prep",
    )(xa, g.reshape(1, d), mods)


def _mm_resid_kernel(a_ref, w_ref, b_ref, x_ref, m_ref, o_ref, *, gate_row):
    y = _dot(a_ref[...], w_ref[...]) + b_ref[...]
    o_ref[...] = x_ref[...] + m_ref[0][gate_row:gate_row + 1] * y


def _mm_resid(a, w, b, xa, mods, gate_row):
    t, d = xa.shape
    k = a.shape[1]
    return pl.pallas_call(
        functools.partial(_mm_resid_kernel, gate_row=gate_row),
        grid=(t // TM,),
        in_specs=[pl.BlockSpec((TM, k), lambda i: (i, 0)),
                  pl.BlockSpec((k, d), lambda i: (0, 0)),
                  pl.BlockSpec((1, d), lambda i: (0, 0)),
                  pl.BlockSpec((TM, d), lambda i: (i, 0)),
                  pl.BlockSpec((1, 6, d), lambda i: (_mod_row(i), 0, 0))],
        out_specs=pl.BlockSpec((TM, d), lambda i: (i, 0)),
        out_shape=jax.ShapeDtypeStruct((t, d), F32),
        compiler_params=_params(("parallel",)),
        name="mm_resid",
    )(a, w, b.reshape(1, d), xa, mods)


def _dft_tables(n1, n2):
    n = n1 * n2
    k1 = np.arange(n1, dtype=np.float64)
    t = (np.arange(n1, dtype=np.float64)[None, :] * n2 + np.arange(n2, dtype=np.float64)[:, None])
    ang = 2.0 * np.pi * k1[None, :, None] * t[:, None, :] / n
    g = np.concatenate([np.cos(ang), -np.sin(ang)], axis=1) / math.sqrt(n1)
    a2 = 2.0 * np.pi * np.outer(np.arange(n2), np.arange(n2)) / n2
    c2, s2 = np.cos(a2) / math.sqrt(n2), np.sin(a2) / math.sqrt(n2)
    f2 = np.block([[c2, s2], [-s2, c2]])
    return jnp.asarray(g, BF16), jnp.asarray(f2, BF16)


def _dft_stage1_kernel(x_ref, g_ref, re_ref, im_ref):
    n1 = x_ref.shape[0]
    a = _dot(g_ref[0], x_ref[...])
    re_ref[0] = a[:n1].astype(re_ref.dtype)
    im_ref[0] = a[n1:].astype(im_ref.dtype)


def _dft_stage2_kernel(re_ref, im_ref, f_ref, zre_ref, zim_ref):
    n2 = re_ref.shape[0]
    z = _dot(f_ref[...], jnp.concatenate([re_ref[...], im_ref[...]], axis=0))
    zre_ref[...] = z[:n2].astype(zre_ref.dtype)
    zim_ref[...] = z[n2:].astype(zim_ref.dtype)


def _pos_dft(h):
    n, d = h.shape
    if n <= 512:
        n1, n2 = n, 1
    else:
        n1, n2 = n // DFT_N2, DFT_N2
    g, f2 = _dft_tables(n1, n2)
    are, aim = pl.pallas_call(
        _dft_stage1_kernel,
        grid=(n2,),
        in_specs=[pl.BlockSpec((n1, d), lambda j: (0, j)),
                  pl.BlockSpec((1, 2 * n1, n1), lambda j: (j, 0, 0))],
        out_specs=[pl.BlockSpec((1, n1, d), lambda j: (j, 0, 0))] * 2,
        out_shape=[jax.ShapeDtypeStruct((n2, n1, d), BF16)] * 2,
        compiler_params=_params(("parallel",)),
        name="dft_stage1",
    )(h.reshape(n1, n2 * d), g)
    if n2 == 1:
        return are.reshape(n, d), aim.reshape(n, d)
    tn = 2048
    zre, zim = pl.pallas_call(
        _dft_stage2_kernel,
        grid=(n1 * d // tn,),
        in_specs=[pl.BlockSpec((n2, tn), lambda j: (0, j)),
                  pl.BlockSpec((n2, tn), lambda j: (0, j)),
                  pl.BlockSpec((2 * n2, 2 * n2), lambda j: (0, 0))],
        out_specs=[pl.BlockSpec((n2, tn), lambda j: (0, j))] * 2,
        out_shape=[jax.ShapeDtypeStruct((n2, n1 * d), BF16)] * 2,
        compiler_params=_params(("parallel",)),
        name="dft_stage2",
    )(are.reshape(n2, n1 * d), aim.reshape(n2, n1 * d), f2)
    return zre.reshape(n, d), zim.reshape(n, d)


def _fourier_out_kernel(zrc_ref, zic_ref, zrl_ref, zil_ref, cc_ref, sc_ref, w_ref, b_ref, x_ref, m_ref, o_ref):
    is_ctx = pl.program_id(0) == 0
    zre = jnp.where(is_ctx, zrc_ref[...], zrl_ref[...])
    zim = jnp.where(is_ctx, zic_ref[...], zil_ref[...])
    gd = FOURIER_GROUP_DIM
    parts = []
    for g in range(FOURIER_GROUPS):
        sl = slice(g * gd, (g + 1) * gd)
        parts.append(_dot(zre[:, sl], cc_ref[...]) + _dot(zim[:, sl], sc_ref[...]))
    f = jnp.concatenate(parts, axis=1).astype(BF16)
    y = _dot(f, w_ref[...]) + b_ref[...]
    o_ref[...] = x_ref[...] + m_ref[0][2:3] * y


def _fourier_layer(xa, mods, g1, w_out, b_out):
    t, d = xa.shape
    h = _prep(xa, g1, mods, 0)
    zrc, zic = _pos_dft(h[:CTX_LEN])
    zrl, zil = _pos_dft(h[CTX_LEN:])
    gd = FOURIER_GROUP_DIM
    ang = 2.0 * np.pi * np.outer(np.arange(gd), np.arange(gd)) / gd
    cc = jnp.asarray(np.cos(ang) / math.sqrt(gd), BF16)
    sc = jnp.asarray(np.sin(ang) / math.sqrt(gd), BF16)
    lat = lambda i: (jnp.maximum(i - 1, 0), 0)
    return pl.pallas_call(
        _fourier_out_kernel,
        grid=(t // TM,),
        in_specs=[pl.BlockSpec((TM, d), lambda i: (0, 0)),
                  pl.BlockSpec((TM, d), lambda i: (0, 0)),
                  pl.BlockSpec((TM, d), lat),
                  pl.BlockSpec((TM, d), lat),
                  pl.BlockSpec((gd, gd), lambda i: (0, 0)),
                  pl.BlockSpec((gd, gd), lambda i: (0, 0)),
                  pl.BlockSpec((d, d), lambda i: (0, 0)),
                  pl.BlockSpec((1, d), lambda i: (0, 0)),
                  pl.BlockSpec((TM, d), lambda i: (i, 0)),
                  pl.BlockSpec((1, 6, d), lambda i: (_mod_row(i), 0, 0))],
        out_specs=pl.BlockSpec((TM, d), lambda i: (i, 0)),
        out_shape=jax.ShapeDtypeStruct((t, d), F32),
        compiler_params=_params(("parallel",)),
        name="fourier_out",
    )(zrc, zic, zrl, zil, cc, sc, w_out.astype(BF16), b_out.reshape(1, d), xa, mods)


def _conv_in_kernel(x_ref, g_ref, m_ref, wb_ref, wc_ref, wv_ref, gb_ref, p_ref):
    m = m_ref[0]
    h = _modulate(x_ref[...], g_ref[...], m[0:1], m[1:2]).astype(BF16)
    gb_ref[...] = _dot(h, wb_ref[...]).astype(gb_ref.dtype)
    p_ref[...] = (_dot(h, wc_ref[...]) * _dot(h, wv_ref[...])).astype(p_ref.dtype)


def _conv_out_kernel(gb_ref, p_ref, pp_ref, pn_ref, k_ref, cb_ref, w_ref, x_ref, m_ref, o_ref):
    i = pl.program_id(0)
    last = pl.num_programs(0) - 1
    p = p_ref[...].astype(F32)
    rows = lax.broadcasted_iota(jnp.int32, p.shape, 0)
    prev_row = jnp.where(i >= 2, pp_ref[7:8, :].astype(F32), 0.0)
    next_row = jnp.where(jnp.logical_and(i >= 1, i < last), pn_ref[0:1, :].astype(F32), 0.0)
    p_dn = jnp.where(rows == 0, prev_row, pltpu.roll(p, 1, 0))
    p_up = jnp.where(rows == TM - 1, next_row, pltpu.roll(p, TM - 1, 0))
    z = k_ref[0:1, :] * p_dn + k_ref[1:2, :] * p + k_ref[2:3, :] * p_up + cb_ref[...]
    q = (gb_ref[...].astype(F32) * z).astype(BF16)
    o_ref[...] = x_ref[...] + m_ref[0][2:3] * _dot(q, w_ref[...])


def _conv_layer(xa, mods, g1, w_in, conv_k, conv_b, w_out):
    t, d = xa.shape
    tn = 1024
    nt = d // tn
    w_in = w_in.astype(BF16)
    gb, p = pl.pallas_call(
        _conv_in_kernel,
        grid=(nt, t // TM),
        in_specs=[pl.BlockSpec((TM, d), lambda j, i: (i, 0)),
                  pl.BlockSpec((1, d), lambda j, i: (0, 0)),
                  pl.BlockSpec((1, 6, d), lambda j, i: (_mod_row(i), 0, 0)),
                  pl.BlockSpec((d, tn), lambda j, i: (0, j)),
                  pl.BlockSpec((d, tn), lambda j, i: (0, nt + j)),
                  pl.BlockSpec((d, tn), lambda j, i: (0, 2 * nt + j))],
        out_specs=[pl.BlockSpec((TM, tn), lambda j, i: (i, j))] * 2,
        out_shape=[jax.ShapeDtypeStruct((t, d), BF16)] * 2,
        compiler_params=_params(("parallel", "parallel")),
        name="conv_in",
    )(xa, g1.reshape(1, d), mods, w_in, w_in, w_in)
    r8 = TM // 8
    n8 = t // 8
    return pl.pallas_call(
        _conv_out_kernel,
        grid=(t // TM,),
        in_specs=[pl.BlockSpec((TM, d), lambda i: (i, 0)),
                  pl.BlockSpec((TM, d), lambda i: (i, 0)),
                  pl.BlockSpec((8, d), lambda i: (jnp.maximum(i * r8 - 1, 0), 0)),
                  pl.BlockSpec((8, d), lambda i: (jnp.minimum((i + 1) * r8, n8 - 1), 0)),
                  pl.BlockSpec((3, d), lambda i: (0, 0)),
                  pl.BlockSpec((1, d), lambda i: (0, 0)),
                  pl.BlockSpec((d, d), lambda i: (0, 0)),
                  pl.BlockSpec((TM, d), lambda i: (i, 0)),
                  pl.BlockSpec((1, 6, d), lambda i: (_mod_row(i), 0, 0))],
        out_specs=pl.BlockSpec((TM, d), lambda i: (i, 0)),
        out_shape=jax.ShapeDtypeStruct((t, d), F32),
        compiler_params=_params(("parallel",)),
        name="conv_out",
    )(gb, p, p, p, conv_k, conv_b.reshape(1, d), w_out.astype(BF16), xa, mods)


def _rope_tables(n_lat):
    rows = n_lat // GRID_W
    r = jnp.repeat(jnp.arange(rows, dtype=F32), GRID_W)
    col = jnp.tile(jnp.arange(GRID_W, dtype=F32), rows)
    inv = ROPE_THETA ** (-jnp.arange(ROPE_AXIS_PAIRS, dtype=F32) / ROPE_AXIS_PAIRS)
    ang = jnp.concatenate([r[:, None] * inv, col[:, None] * inv], axis=-1)
    cos, sin = jnp.cos(ang), jnp.sin(ang)
    cos_f = jnp.concatenate([cos, cos], axis=-1)
    sin_f = jnp.concatenate([-sin, sin], axis=-1)
    cos_f = jnp.concatenate([jnp.ones((CTX_LEN, HEAD_DIM), F32), cos_f], axis=0)
    sin_f = jnp.concatenate([jnp.zeros((CTX_LEN, HEAD_DIM), F32), sin_f], axis=0)
    return cos_f, sin_f


def _qkv_kernel(x_ref, g_ref, m_ref, w_ref, qg_ref, kg_ref, cos_ref, sin_ref, q_ref, k_ref, v_ref):
    m = m_ref[0]
    h = _modulate(x_ref[...], g_ref[...], m[0:1], m[1:2]).astype(BF16)
    a = _dot(h, w_ref[...])
    cos, sin = cos_ref[...], sin_ref[...]
    nq = N_HEADS * HEAD_DIM
    nk = N_KV_HEADS * HEAD_DIM

    def norm_rope(u, gain):
        ms = jnp.mean(u * u, axis=-1, keepdims=True)
        un = u * lax.rsqrt(ms + NORM_EPS) * gain
        return un * cos + pltpu.roll(un, HEAD_DIM // 2, 1) * sin

    for hh in range(N_HEADS):
        sl = slice(hh * HEAD_DIM, (hh + 1) * HEAD_DIM)
        q_ref[:, sl] = (norm_rope(a[:, sl], qg_ref[...]) * HEAD_DIM ** -0.5).astype(q_ref.dtype)
    for hh in range(N_KV_HEADS):
        sl = slice(hh * HEAD_DIM, (hh + 1) * HEAD_DIM)
        k_ref[:, sl] = norm_rope(a[:, nq + hh * HEAD_DIM:nq + (hh + 1) * HEAD_DIM], kg_ref[...]).astype(k_ref.dtype)
    v_ref[...] = a[:, nq + nk:].astype(v_ref.dtype)


def _attn_kernel(q_ref, k_ref, v_ref, o_ref, m_ref, l_ref, acc_ref):
    i = pl.program_id(1)
    q4 = jnp.concatenate([q_ref[:, j * HEAD_DIM:(j + 1) * HEAD_DIM] for j in range(Q_PER_KV)], axis=0)
    m_ref[...] = jnp.full(m_ref.shape, -jnp.inf, F32)
    l_ref[...] = jnp.zeros(l_ref.shape, F32)
    acc_ref[...] = jnp.zeros(acc_ref.shape, F32)

    def step(start, size):
        kc = k_ref[pl.ds(start, size), :]
        vc = v_ref[pl.ds(start, size), :]
        s = lax.dot_general(q4, kc, (((1,), (1,)), ((), ())), preferred_element_type=F32)
        m_old = m_ref[...]
        m_new = jnp.maximum(m_old, jnp.max(s, axis=-1, keepdims=True))
        alpha = jnp.exp(m_old - m_new)
        p = jnp.exp(s - m_new)
        l_ref[...] = alpha * l_ref[...] + jnp.sum(p, axis=-1, keepdims=True)
        acc_ref[...] = alpha * acc_ref[...] + _dot(p.astype(BF16), vc)
        m_ref[...] = m_new

    step(0, CTX_LEN)

    @pl.when(i > 0)
    def _():
        n_steps = (k_ref.shape[0] - CTX_LEN) // ATT_TK

        def body(c, carry):
            step(pl.multiple_of(CTX_LEN + c * ATT_TK, ATT_TK // 2), ATT_TK)
            return carry

        lax.fori_loop(0, n_steps, body, 0)

    o = acc_ref[...] / l_ref[...]
    for j in range(Q_PER_KV):
        o_ref[:, j * HEAD_DIM:(j + 1) * HEAD_DIM] = o[j * TM:(j + 1) * TM].astype(o_ref.dtype)


def _attn_layer(xa, mods, g1, wq, wk, wv, wo, q_g, k_g):
    t, d = xa.shape
    nq, nk = N_HEADS * HEAD_DIM, N_KV_HEADS * HEAD_DIM
    w = jnp.concatenate([wq, wk, wv], axis=1).astype(BF16)
    cos_f, sin_f = _rope_tables(t - CTX_LEN)
    q, k, v = pl.pallas_call(
        _qkv_kernel,
        grid=(t // TM,),
        in_specs=[pl.BlockSpec((TM, d), lambda i: (i, 0)),
                  pl.BlockSpec((1, d), lambda i: (0, 0)),
                  pl.BlockSpec((1, 6, d), lambda i: (_mod_row(i), 0, 0)),
                  pl.BlockSpec((d, nq + 2 * nk), lambda i: (0, 0)),
                  pl.BlockSpec((1, HEAD_DIM), lambda i: (0, 0)),
                  pl.BlockSpec((1, HEAD_DIM), lambda i: (0, 0)),
                  pl.BlockSpec((TM, HEAD_DIM), lambda i: (i, 0)),
                  pl.BlockSpec((TM, HEAD_DIM), lambda i: (i, 0))],
        out_specs=[pl.BlockSpec((TM, nq), lambda i: (i, 0)),
                   pl.BlockSpec((TM, nk), lambda i: (i, 0)),
                   pl.BlockSpec((TM, nk), lambda i: (i, 0))],
        out_shape=[jax.ShapeDtypeStruct((t, nq), BF16),
                   jax.ShapeDtypeStruct((t, nk), BF16),
                   jax.ShapeDtypeStruct((t, nk), BF16)],
        compiler_params=_params(("parallel",)),
        name="qkv",
    )(xa, g1.reshape(1, d), mods, w, q_g.reshape(1, HEAD_DIM), k_g.reshape(1, HEAD_DIM), cos_f, sin_f)
    qw = Q_PER_KV * HEAD_DIM
    o = pl.pallas_call(
        _attn_kernel,
        grid=(N_KV_HEADS, t // TM),
        in_specs=[pl.BlockSpec((TM, qw), lambda n, i: (i, n)),
                  pl.BlockSpec((t, HEAD_DIM), lambda n, i: (0, n)),
                  pl.BlockSpec((t, HEAD_DIM), lambda n, i: (0, n))],
        out_specs=pl.BlockSpec((TM, qw), lambda n, i: (i, n)),
        out_shape=jax.ShapeDtypeStruct((t, nq), BF16),
        scratch_shapes=[pltpu.VMEM((Q_PER_KV * TM, 1), F32),
                        pltpu.VMEM((Q_PER_KV * TM, 1), F32),
                        pltpu.VMEM((Q_PER_KV * TM, HEAD_DIM), F32)],
        compiler_params=_params(("parallel", "parallel")),
        name="attention",
    )(q, k, v)
    return _mm_resid(o, wo.astype(BF16), jnp.zeros((d,), F32), xa, mods, 2)


def _s5_matrices(a_re, a_im, log_dt, b_re, b_im, c_re, c_im):
    l, g, p, c = SSM_SUB, SSM_GROUPS, SSM_STATE, SSM_GROUP_DIM
    a = lax.complex(a_re.astype(F32), a_im.astype(F32))
    adt = a * jnp.exp(log_dt.astype(F32))[:, None]
    a_bar = jnp.exp(adt)
    b_bar = ((a_bar - 1) / a)[..., None] * lax.complex(b_re.astype(F32), b_im.astype(F32))
    c_mat = lax.complex(c_re.astype(F32), c_im.astype(F32))
    pw = jnp.exp(adt[:, None, :] * jnp.arange(l + 1, dtype=F32)[None, :, None])
    kern = jnp.einsum('gcp,gkp,gpd->gkcd', c_mat, pw[:, :l], b_bar,
                      precision=lax.Precision.HIGHEST).real
    lag = jnp.arange(l)[None, :] - jnp.arange(l)[:, None]
    m = jnp.where((lag >= 0)[None, :, :, None, None], kern[:, jnp.clip(lag, 0, l - 1)], 0.0)
    m = jnp.transpose(m, (0, 1, 4, 2, 3))
    e = pw[:, l - 1 - jnp.arange(l), :][:, :, None, :] * jnp.transpose(b_bar, (0, 2, 1))[:, None, :, :]
    cp = c_mat[:, None, :, :] * pw[:, 1:, None, :]
    cp = jnp.transpose(cp, (0, 3, 1, 2))
    eye = jnp.eye(SSM_GPB, dtype=F32)

    def blockdiag(x, pattern):
        return jnp.einsum(pattern, x.reshape((SSM_CB, SSM_GPB) + x.shape[1:]), eye)

    mbig = blockdiag(m, 'bisctd,ij->bsictjd').reshape(SSM_CB, l * LANES, l * LANES)
    e_re = blockdiag(e.real, 'biscp,ij->bsicjp').reshape(SSM_CB, l * LANES, SSM_GPB * p)
    e_im = blockdiag(e.imag, 'biscp,ij->bsicjp').reshape(SSM_CB, l * LANES, SSM_GPB * p)
    cp_re = blockdiag(cp.real, 'biptd,ij->biptjd').reshape(SSM_CB, SSM_GPB * p, l * LANES)
    cp_im = blockdiag(-cp.imag, 'biptd,ij->biptjd').reshape(SSM_CB, SSM_GPB * p, l * LANES)
    w1 = jnp.concatenate([e_re, e_im], axis=2).astype(BF16)
    w2 = jnp.concatenate([mbig, cp_re, cp_im], axis=1).astype(BF16)
    al = pw[:, l].reshape(SSM_CB, 1, SSM_GPB * p)
    return w1, w2, al.real, al.imag


def _s5_scan_kernel(*refs):
    l = SSM_SUB
    u_refs = refs[:l]
    w1_ref, w2_ref, ar_ref, ai_ref = refs[l:l + 4]
    y_refs = refs[l + 4:2 * l + 4]
    v_ref, h_ref = refs[2 * l + 4:]
    ns = ar_ref.shape[-1]
    rows = v_ref.shape[0]

    @pl.when(pl.program_id(1) == 0)
    def _():
        h_ref[...] = jnp.zeros(h_ref.shape, F32)

    u = jnp.concatenate([r[...] for r in u_refs], axis=1)
    v_ref[...] = _dot(u, w1_ref[0])
    ar, ai = ar_ref[0], ai_ref[0]

    def body(j, carry):
        hr, hi = carry
        vr = v_ref[pl.ds(j, 1), 0:ns]
        vi = v_ref[pl.ds(j, 1), ns:2 * ns]
        v_ref[pl.ds(j, 1), 0:ns] = hr
        v_ref[pl.ds(j, 1), ns:2 * ns] = hi
        return ar * hr - ai * hi + vr, ar * hi + ai * hr + vi

    hr, hi = lax.fori_loop(0, rows, body, (h_ref[0:1, :], h_ref[1:2, :]))
    h_ref[0:1, :] = hr
    h_ref[1:2, :] = hi
    y = _dot(jnp.concatenate([u, v_ref[...].astype(BF16)], axis=1), w2_ref[0])
    for t in range(l):
        y_refs[t][...] = y[:, t * LANES:(t + 1) * LANES]


def _s5_scan(seq, mats):
    t, d = seq.shape
    l = SSM_SUB
    w1, w2, al_re, al_im = mats
    n_sub = t // l
    rows = max(r for r in range(8, min(n_sub, 520) + 1, 8) if n_sub % r == 0)
    ns = SSM_GPB * SSM_STATE
    u2 = seq.reshape(n_sub, l * d)
    blk = lambda s: pl.BlockSpec((rows, LANES), lambda cb, r, s=s: (r, s * SSM_CB + cb))
    ys = pl.pallas_call(
        _s5_scan_kernel,
        grid=(SSM_CB, n_sub // rows),
        in_specs=[blk(s) for s in range(l)] + [
            pl.BlockSpec((1,) + w1.shape[1:], lambda cb, r: (cb, 0, 0)),
            pl.BlockSpec((1,) + w2.shape[1:], lambda cb, r: (cb, 0, 0)),
            pl.BlockSpec((1, 1, ns), lambda cb, r: (cb, 0, 0)),
            pl.BlockSpec((1, 1, ns), lambda cb, r: (cb, 0, 0))],
        out_specs=[pl.BlockSpec((rows, LANES), lambda cb, r: (r, cb))] * l,
        out_shape=[jax.ShapeDtypeStruct((n_sub, d), F32)] * l,
        scratch_shapes=[pltpu.VMEM((rows, 2 * ns), F32), pltpu.VMEM((2, ns), F32)],
        compiler_params=_params(("parallel", "arbitrary")),
        name="s5_scan",
    )(*([u2] * l), w1, w2, al_re, al_im)
    return jnp.stack(ys, axis=1).reshape(t, d)


def _gelu_tanh(x):
    return 0.5 * x * (1.0 + jnp.tanh(math.sqrt(2.0 / math.pi) * (x + 0.044715 * x * x * x)))


def _s5_glu_kernel(x_ref, g_ref, m_ref, d_ref, yf_ref, yb_ref, wa_ref, wg_ref, ba_ref, bg_ref, xo_ref, mo_ref, o_ref):
    m = m_ref[0]
    h = _modulate(x_ref[...], g_ref[...], m[0:1], m[1:2])
    z = _gelu_tanh(d_ref[...] * h + yf_ref[...] + yb_ref[...]).astype(BF16)
    a = _dot(z, wa_ref[...]) + ba_ref[...]
    g = _dot(z, wg_ref[...]) + bg_ref[...]
    o_ref[...] = xo_ref[...] + mo_ref[0][2:3] * (a * jax.nn.sigmoid(g))


def _s5_layer(xa, mods, g1, a_re, a_im, log_dt, b_re, b_im, c_re, c_im, d_skip, glu_w, glu_b):
    t, d = xa.shape
    h = _prep(xa, g1, mods, 0)
    ys = []
    for direction in range(2):
        mats = _s5_matrices(a_re[direction], a_im[direction], log_dt[direction], b_re[direction], b_im[direction],
                            c_re[direction], c_im[direction])
        if direction == 0:
            ys.append(_s5_scan(h, mats))
        else:
            seq = jnp.concatenate([jnp.flip(h[:CTX_LEN], 0), jnp.flip(h[CTX_LEN:], 0)], axis=0)
            y = _s5_scan(seq, mats)
            ys.append(jnp.concatenate([jnp.flip(y[:CTX_LEN], 0), jnp.flip(y[CTX_LEN:], 0)], axis=0))
    tn = 1024
    nt = d // tn
    glu_w = glu_w.astype(BF16)
    return pl.pallas_call(
        _s5_glu_kernel,
        grid=(nt, t // TM),
        in_specs=[pl.BlockSpec((TM, d), lambda j, i: (i, 0)),
                  pl.BlockSpec((1, d), lambda j, i: (0, 0)),
                  pl.BlockSpec((1, 6, d), lambda j, i: (_mod_row(i), 0, 0)),
                  pl.BlockSpec((1, d), lambda j, i: (0, 0)),
                  pl.BlockSpec((TM, d), lambda j, i: (i, 0)),
                  pl.BlockSpec((TM, d), lambda j, i: (i, 0)),
                  pl.BlockSpec((d, tn), lambda j, i: (0, j)),
                  pl.BlockSpec((d, tn), lambda j, i: (0, nt + j)),
                  pl.BlockSpec((1, tn), lambda j, i: (0, j)),
                  pl.BlockSpec((1, tn), lambda j, i: (0, nt + j)),
                  pl.BlockSpec((TM, tn), lambda j, i: (i, j)),
                  pl.BlockSpec((1, 6, tn), lambda j, i: (_mod_row(i), 0, j))],
        out_specs=pl.BlockSpec((TM, tn), lambda j, i: (i, j)),
        out_shape=jax.ShapeDtypeStruct((t, d), F32),
        compiler_params=_params(("parallel", "parallel")),
        name="s5_glu",
    )(xa, g1.reshape(1, d), mods, d_skip.reshape(1, d), ys[0], ys[1], glu_w, glu_w,
      glu_b.reshape(1, 2 * d), glu_b.reshape(1, 2 * d), xa, mods)


def _router_kernel(x_ref, g_ref, m_ref, wr_ref, br_ref, tri_ref, h_ref, idx_ref, gate_ref, rank_ref, cnt_ref, base_ref):
    i = pl.program_id(0)

    @pl.when(i == 0)
    def _():
        base_ref[...] = jnp.zeros(base_ref.shape, F32)

    m = m_ref[0]
    h = _modulate(x_ref[...], g_ref[...], m[3:4], m[4:5])
    h_ref[...] = h.astype(h_ref.dtype)
    logits = lax.dot_general(wr_ref[...], h, (((1,), (1,)), ((), ())), precision=lax.Precision.HIGHEST,
                             preferred_element_type=F32) + br_ref[...]
    e_iota = lax.broadcasted_iota(jnp.int32, logits.shape, 0).astype(F32)
    base = base_ref[...]
    tops, idxs, ranks = [], [], []
    for _ in range(TOP_K):
        top = jnp.max(logits, axis=0, keepdims=True)
        idx = jnp.min(jnp.where(logits == top, e_iota, float(N_EXPERTS)), axis=0, keepdims=True)
        sel = e_iota == idx
        logits = jnp.where(sel, -jnp.inf, logits)
        onehot = sel.astype(BF16)
        before = _dot(onehot, tri_ref[...])
        ranks.append(jnp.sum(jnp.where(sel, base + before, 0.0), axis=0, keepdims=True))
        base = base + jnp.sum(sel.astype(F32), axis=1, keepdims=True)
        tops.append(top)
        idxs.append(idx)
    base_ref[...] = base
    cnt_ref[...] = jnp.broadcast_to(base, cnt_ref.shape)
    ex = [jnp.exp(v - tops[0]) for v in tops]
    den = ex[0] + ex[1] + ex[2] + ex[3]
    idx_ref[...] = jnp.concatenate(idxs, axis=0).astype(jnp.int32)
    gate_ref[...] = jnp.concatenate([v / den for v in ex], axis=0)
    rank_ref[...] = jnp.concatenate(ranks, axis=0).astype(jnp.int32)


def _ffn_kernel(be_ref, x_ref, w1_ref, b1_ref, w2_ref, b2_ref, o_ref):
    i = pl.program_id(0)

    @pl.when(be_ref[1, i] > 0)
    def _():
        h = _dot(x_ref[...], w1_ref[0]) + b1_ref[0]
        gate = jnp.minimum(h[:, :D_EXPERT], SWIGLU_LIMIT)
        lin = jnp.clip(h[:, D_EXPERT:], -SWIGLU_LIMIT, SWIGLU_LIMIT)
        act = ((lin + 1.0) * gate * jax.nn.sigmoid(SWIGLU_ALPHA * gate)).astype(BF16)
        o_ref[...] = (_dot(act, w2_ref[0]) + b2_ref[0]).astype(o_ref.dtype)


def _combine_kernel(y_ref, gt_ref, x_ref, m_ref, o_ref):
    gt = gt_ref[...]
    acc = gt[:, 0:1] * y_ref[0].astype(F32)
    for k in range(1, TOP_K):
        acc = acc + gt[:, k:k + 1] * y_ref[k].astype(F32)
    o_ref[...] = x_ref[...] + m_ref[0][5:6] * acc


def _moe_layer(xa, mods, g2, w_r, b_r, w1, b1, w2, b2):
    t, d = xa.shape
    nb = t // TM
    tri = jnp.asarray(np.triu(np.ones((TM, TM), np.float32), 1), BF16)
    h2, idx, gates, rank, cnt = pl.pallas_call(
        _router_kernel,
        grid=(nb,),
        in_specs=[pl.BlockSpec((TM, d), lambda i: (i, 0)),
                  pl.BlockSpec((1, d), lambda i: (0, 0)),
                  pl.BlockSpec((1, 6, d), lambda i: (_mod_row(i), 0, 0)),
                  pl.BlockSpec((N_EXPERTS, d), lambda i: (0, 0)),
                  pl.BlockSpec((N_EXPERTS, 1), lambda i: (0, 0)),
                  pl.BlockSpec((TM, TM), lambda i: (0, 0))],
        out_specs=[pl.BlockSpec((TM, d), lambda i: (i, 0)),
                   pl.BlockSpec((TOP_K, TM), lambda i: (0, i)),
                   pl.BlockSpec((TOP_K, TM), lambda i: (0, i)),
                   pl.BlockSpec((TOP_K, TM), lambda i: (0, i)),
                   pl.BlockSpec((N_EXPERTS, LANES), lambda i: (0, 0))],
        out_shape=[jax.ShapeDtypeStruct((t, d), BF16),
                   jax.ShapeDtypeStruct((TOP_K, t), jnp.int32),
                   jax.ShapeDtypeStruct((TOP_K, t), F32),
                   jax.ShapeDtypeStruct((TOP_K, t), jnp.int32),
                   jax.ShapeDtypeStruct((N_EXPERTS, LANES), F32)],
        scratch_shapes=[pltpu.VMEM((N_EXPERTS, 1), F32)],
        compiler_params=_params(("arbitrary",)),
        name="router",
    )(xa, g2.reshape(1, d), mods, w_r.T, b_r.reshape(N_EXPERTS, 1), tri)

    counts = cnt[:, 0].astype(jnp.int32)
    padded = (counts + MOE_TM - 1) // MOE_TM * MOE_TM
    pend = jnp.cumsum(padded)
    pstart = pend - padded
    dest = pstart[idx] + rank
    n_rows = -(-t * TOP_K // MOE_TM) * MOE_TM + N_EXPERTS * MOE_TM
    n_blocks = n_rows // MOE_TM
    blk_start = jnp.arange(n_blocks, dtype=jnp.int32) * MOE_TM
    blk_used = (blk_start < pend[-1]).astype(jnp.int32)
    blk_e = jnp.minimum(jnp.searchsorted(pend, blk_start, side='right'), N_EXPERTS - 1).astype(jnp.int32)
    last_e = blk_e[jnp.maximum(pend[-1] // MOE_TM - 1, 0)]
    blk_e = jnp.where(blk_used > 0, blk_e, last_e)
    be = jnp.stack([blk_e, blk_used])
    tok = jnp.broadcast_to(jnp.arange(t, dtype=jnp.int32)[None, :], (TOP_K, t))
    row_tok = jnp.zeros((n_rows,), jnp.int32).at[dest.reshape(-1)].set(tok.reshape(-1))
    xs = jnp.take(h2, row_tok, axis=0)

    ys = pl.pallas_call(
        _ffn_kernel,
        grid_spec=pltpu.PrefetchScalarGridSpec(
            num_scalar_prefetch=1,
            grid=(n_blocks,),
            in_specs=[pl.BlockSpec((MOE_TM, d), lambda i, be: (i, 0)),
                      pl.BlockSpec((1, d, 2 * D_EXPERT), lambda i, be: (be[0, i], 0, 0)),
                      pl.BlockSpec((1, 1, 2 * D_EXPERT), lambda i, be: (be[0, i], 0, 0)),
                      pl.BlockSpec((1, D_EXPERT, d), lambda i, be: (be[0, i], 0, 0)),
                      pl.BlockSpec((1, 1, d), lambda i, be: (be[0, i], 0, 0))],
            out_specs=pl.BlockSpec((MOE_TM, d), lambda i, be: (i, 0))),
        out_shape=jax.ShapeDtypeStruct((n_rows, d), BF16),
        compiler_params=_params(("arbitrary",)),
        name="moe_ffn",
    )(be, xs, w1.astype(BF16), b1.reshape(N_EXPERTS, 1, 2 * D_EXPERT), w2.astype(BF16), b2.reshape(N_EXPERTS, 1, d))

    y4 = jnp.take(ys, dest.reshape(-1), axis=0).reshape(TOP_K, t, d)
    return pl.pallas_call(
        _combine_kernel,
        grid=(nb,),
        in_specs=[pl.BlockSpec((TOP_K, TM, d), lambda i: (0, i, 0)),
                  pl.BlockSpec((TM, TOP_K), lambda i: (i, 0)),
                  pl.BlockSpec((TM, d), lambda i: (i, 0)),
                  pl.BlockSpec((1, 6, d), lambda i: (_mod_row(i), 0, 0))],
        out_specs=pl.BlockSpec((TM, d), lambda i: (i, 0)),
        out_shape=jax.ShapeDtypeStruct((t, d), F32),
        compiler_params=_params(("parallel",)),
        name="moe_combine",
    )(y4, gates.T, xa, mods)


def kernel(x, c, ctx, c_ctx, ada_w, ada_b, norm1_g, norm2_g, fourier_w_out, fourier_b_out, conv_w_in, conv_k, conv_b, conv_w_out, attn_wq, attn_wk, attn_wv, attn_wo, attn_q_g, attn_k_g, ssm_a_re, ssm_a_im, ssm_log_dt, ssm_b_re, ssm_b_im, ssm_c_re, ssm_c_im, ssm_d, ssm_glu_w, ssm_glu_b, router_w, router_b, moe_w1, moe_b1, moe_w2, moe_b2):
    bsz, n, d = x.shape
    assert bsz == 1 and d == D_MODEL and ctx.shape[1] == CTX_LEN and n % TM == 0
    depth = ada_w.shape[0]
    cond8 = jnp.zeros((8, d), F32).at[0].set(c_ctx).at[1].set(c[0])
    mods = _ada_all(cond8, ada_w, ada_b).reshape(depth, 8, 6, d)
    xa = jnp.concatenate([ctx[0], x[0]], axis=0)
    for i in range(depth):
        m, j = i % 4, i // 4
        if m == 0:
            xa = _fourier_layer(xa, mods[i], norm1_g[i], fourier_w_out[j], fourier_b_out[j])
        elif m == 1:
            xa = _conv_layer(xa, mods[i], norm1_g[i], conv_w_in[j], conv_k[j], conv_b[j], conv_w_out[j])
        elif m == 2:
            xa = _attn_layer(xa, mods[i], norm1_g[i], attn_wq[j], attn_wk[j], attn_wv[j], attn_wo[j],
                             attn_q_g[j], attn_k_g[j])
        else:
            xa = _s5_layer(xa, mods[i], norm1_g[i], ssm_a_re[j], ssm_a_im[j], ssm_log_dt[j], ssm_b_re[j],
                           ssm_b_im[j], ssm_c_re[j], ssm_c_im[j], ssm_d[j], ssm_glu_w[j], ssm_glu_b[j])
        xa = _moe_layer(xa, mods[i], norm2_g[i], router_w[i], router_b[i], moe_w1[i], moe_b1[i], moe_w2[i], moe_b2[i])
    return xa[CTX_LEN:][None]
```

```python
import functools
import math

import numpy as np
import jax
import jax.numpy as jnp
from jax import lax
from jax.experimental import pallas as pl
from jax.experimental.pallas import tpu as pltpu

F32 = jnp.float32
BF16 = jnp.bfloat16

D_MODEL = 2048
GRID_W = 64
CTX_LEN = 256
FOURIER_GROUPS = 8
FOURIER_GROUP_DIM = D_MODEL // FOURIER_GROUPS
HEAD_DIM = 128
N_HEADS = D_MODEL // HEAD_DIM
N_KV_HEADS = 4
Q_PER_KV = N_HEADS // N_KV_HEADS
ROPE_AXIS_PAIRS = HEAD_DIM // 4
ROPE_THETA = 10000.0
SSM_GROUP_DIM = 16
SSM_GROUPS = D_MODEL // SSM_GROUP_DIM
SSM_STATE = 64
N_EXPERTS = 32
TOP_K = 4
D_EXPERT = 1024
SWIGLU_LIMIT = 7.0
SWIGLU_ALPHA = 1.702
NORM_EPS = 1e-6

LANES = 128
TM = 256
DFT_N2 = 128
SSM_SUB = 8
SSM_CB = D_MODEL // LANES
SSM_GPB = LANES // SSM_GROUP_DIM
MOE_TM = 512
ATT_TK = 512
WCAST_ROWS = 256
VMEM_LIMIT = 56 * 2 ** 20


def _params(sem, vmem=VMEM_LIMIT):
    return pltpu.CompilerParams(dimension_semantics=sem, vmem_limit_bytes=vmem)


def _mod_row(i):
    return jnp.minimum(i, 1)


def _modulate(x, g, shift, scale):
    ms = jnp.mean(x * x, axis=-1, keepdims=True)
    return (x * lax.rsqrt(ms + NORM_EPS) * g) * (1.0 + scale) + shift


def _dot(a, b):
    return jnp.dot(a, b, preferred_element_type=F32)


def _ada_kernel(c_ref, w_ref, b_ref, o_ref):
    c = c_ref[...]
    s = c * jax.nn.sigmoid(c)
    o_ref[0] = jnp.dot(s, w_ref[0], precision=lax.Precision.HIGHEST, preferred_element_type=F32) + b_ref[0]


def _ada_all(cond8, ada_w, ada_b):
    depth, d, n6 = ada_w.shape
    tn = 1024
    return pl.pallas_call(
        _ada_kernel,
        grid=(depth, n6 // tn),
        in_specs=[pl.BlockSpec((8, d), lambda l, j: (0, 0)),
                  pl.BlockSpec((1, d, tn), lambda l, j: (l, 0, j)),
                  pl.BlockSpec((1, 1, tn), lambda l, j: (l, 0, j))],
        out_specs=pl.BlockSpec((1, 8, tn), lambda l, j: (l, 0, j)),
        out_shape=jax.ShapeDtypeStruct((depth, 8, n6), F32),
        compiler_params=_params(("parallel", "parallel")),
        name="ada",
    )(cond8, ada_w, ada_b.reshape(depth, 1, n6))


def _prep_kernel(x_ref, g_ref, m_ref, o_ref, *, which):
    m = m_ref[0]
    h = _modulate(x_ref[...], g_ref[...], m[3 * which:3 * which + 1], m[3 * which + 1:3 * which + 2])
    o_ref[...] = h.astype(o_ref.dtype)


def _prep(xa, g, mods, which):
    t, d = xa.shape
    return pl.pallas_call(
        functools.partial(_prep_kernel, which=which),
        grid=(t // TM,),
        in_specs=[pl.BlockSpec((TM, d), lambda i: (i, 0)),
                  pl.BlockSpec((1, d), lambda i: (0, 0)),
                  pl.BlockSpec((1, 6, d), lambda i: (_mod_row(i), 0, 0))],
        out_specs=pl.BlockSpec((TM, d), lambda i: (i, 0)),
        out_shape=jax.ShapeDtypeStruct((t, d), BF16),
        compiler_params=_params(("parallel",)),
        name="prep",
    )(xa, g.reshape(1, d), mods)


def _mm_resid_kernel(a_ref, w_ref, b_ref, x_ref, m_ref, o_ref, *, gate_row):
    y = _dot(a_ref[...], w_ref[...]) + b_ref[...]
    o_ref[...] = x_ref[...] + m_ref[0][gate_row:gate_row + 1] * y


def _mm_resid(a, w, b, xa, mods, gate_row):
    t, d = xa.shape
    k = a.shape[1]
    return pl.pallas_call(
        functools.partial(_mm_resid_kernel, gate_row=gate_row),
        grid=(t // TM,),
        in_specs=[pl.BlockSpec((TM, k), lambda i: (i, 0)),
                  pl.BlockSpec((k, d), lambda i: (0, 0)),
                  pl.BlockSpec((1, d), lambda i: (0, 0)),
                  pl.BlockSpec((TM, d), lambda i: (i, 0)),
                  pl.BlockSpec((1, 6, d), lambda i: (_mod_row(i), 0, 0))],
        out_specs=pl.BlockSpec((TM, d), lambda i: (i, 0)),
        out_shape=jax.ShapeDtypeStruct((t, d), F32),
        compiler_params=_params(("parallel",)),
        name="mm_resid",
    )(a, w, b.reshape(1, d), xa, mods)


def _dft_tables(n1, n2):
    n = n1 * n2
    k1 = np.arange(n1, dtype=np.float64)
    t = (np.arange(n1, dtype=np.float64)[None, :] * n2 + np.arange(n2, dtype=np.float64)[:, None])
    ang = 2.0 * np.pi * k1[None, :, None] * t[:, None, :] / n
    g = np.concatenate([np.cos(ang), -np.sin(ang)], axis=1) / math.sqrt(n1)
    a2 = 2.0 * np.pi * np.outer(np.arange(n2), np.arange(n2)) / n2
    c2, s2 = np.cos(a2) / math.sqrt(n2), np.sin(a2) / math.sqrt(n2)
    f2 = np.block([[c2, s2], [-s2, c2]])
    return jnp.asarray(g, BF16), jnp.asarray(f2, BF16)


def _dft_stage1_kernel(x_ref, g_ref, re_ref, im_ref):
    n1 = x_ref.shape[0]
    a = _dot(g_ref[0], x_ref[...])
    re_ref[0] = a[:n1].astype(re_ref.dtype)
    im_ref[0] = a[n1:].astype(im_ref.dtype)


def _dft_stage2_kernel(re_ref, im_ref, f_ref, zre_ref, zim_ref):
    n2 = re_ref.shape[0]
    z = _dot(f_ref[...], jnp.concatenate([re_ref[...], im_ref[...]], axis=0))
    zre_ref[...] = z[:n2].astype(zre_ref.dtype)
    zim_ref[...] = z[n2:].astype(zim_ref.dtype)


def _pos_dft(h):
    n, d = h.shape
    if n <= 512:
        n1, n2 = n, 1
    else:
        n1, n2 = n // DFT_N2, DFT_N2
    g, f2 = _dft_tables(n1, n2)
    are, aim = pl.pallas_call(
        _dft_stage1_kernel,
        grid=(n2,),
        in_specs=[pl.BlockSpec((n1, d), lambda j: (0, j)),
                  pl.BlockSpec((1, 2 * n1, n1), lambda j: (j, 0, 0))],
        out_specs=[pl.BlockSpec((1, n1, d), lambda j: (j, 0, 0))] * 2,
        out_shape=[jax.ShapeDtypeStruct((n2, n1, d), BF16)] * 2,
        compiler_params=_params(("parallel",)),
        name="dft_stage1",
    )(h.reshape(n1, n2 * d), g)
    if n2 == 1:
        return are.reshape(n, d), aim.reshape(n, d)
    tn = 2048
    zre, zim = pl.pallas_call(
        _dft_stage2_kernel,
        grid=(n1 * d // tn,),
        in_specs=[pl.BlockSpec((n2, tn), lambda j: (0, j)),
                  pl.BlockSpec((n2, tn), lambda j: (0, j)),
                  pl.BlockSpec((2 * n2, 2 * n2), lambda j: (0, 0))],
        out_specs=[pl.BlockSpec((n2, tn), lambda j: (0, j))] * 2,
        out_shape=[jax.ShapeDtypeStruct((n2, n1 * d), BF16)] * 2,
        compiler_params=_params(("parallel",)),
        name="dft_stage2",
    )(are.reshape(n2, n1 * d), aim.reshape(n2, n1 * d), f2)
    return zre.reshape(n, d), zim.reshape(n, d)


def _fourier_out_kernel(zrc_ref, zic_ref, zrl_ref, zil_ref, cc_ref, sc_ref, w_ref, b_ref, x_ref, m_ref, o_ref):
    is_ctx = pl.program_id(0) == 0
    zre = jnp.where(is_ctx, zrc_ref[...], zrl_ref[...])
    zim = jnp.where(is_ctx, zic_ref[...], zil_ref[...])
    gd = FOURIER_GROUP_DIM
    parts = []
    for g in range(FOURIER_GROUPS):
        sl = slice(g * gd, (g + 1) * gd)
        parts.append(_dot(zre[:, sl], cc_ref[...]) + _dot(zim[:, sl], sc_ref[...]))
    f = jnp.concatenate(parts, axis=1).astype(BF16)
    y = _dot(f, w_ref[...]) + b_ref[...]
    o_ref[...] = x_ref[...] + m_ref[0][2:3] * y


def _fourier_layer(xa, mods, g1, w_out, b_out):
    t, d = xa.shape
    h = _prep(xa, g1, mods, 0)
    zrc, zic = _pos_dft(h[:CTX_LEN])
    zrl, zil = _pos_dft(h[CTX_LEN:])
    gd = FOURIER_GROUP_DIM
    ang = 2.0 * np.pi * np.outer(np.arange(gd), np.arange(gd)) / gd
    cc = jnp.asarray(np.cos(ang) / math.sqrt(gd), BF16)
    sc = jnp.asarray(np.sin(ang) / math.sqrt(gd), BF16)
    lat = lambda i: (jnp.maximum(i - 1, 0), 0)
    return pl.pallas_call(
        _fourier_out_kernel,
        grid=(t // TM,),
        in_specs=[pl.BlockSpec((TM, d), lambda i: (0, 0)),
                  pl.BlockSpec((TM, d), lambda i: (0, 0)),
                  pl.BlockSpec((TM, d), lat),
                  pl.BlockSpec((TM, d), lat),
                  pl.BlockSpec((gd, gd), lambda i: (0, 0)),
                  pl.BlockSpec((gd, gd), lambda i: (0, 0)),
                  pl.BlockSpec((d, d), lambda i: (0, 0)),
                  pl.BlockSpec((1, d), lambda i: (0, 0)),
                  pl.BlockSpec((TM, d), lambda i: (i, 0)),
                  pl.BlockSpec((1, 6, d), lambda i: (_mod_row(i), 0, 0))],
        out_specs=pl.BlockSpec((TM, d), lambda i: (i, 0)),
        out_shape=jax.ShapeDtypeStruct((t, d), F32),
        compiler_params=_params(("parallel",)),
        name="fourier_out",
    )(zrc, zic, zrl, zil, cc, sc, w_out.astype(BF16), b_out.reshape(1, d), xa, mods)


def _conv_in_kernel(x_ref, g_ref, m_ref, wb_ref, wc_ref, wv_ref, gb_ref, p_ref):
    m = m_ref[0]
    h = _modulate(x_ref[...], g_ref[...], m[0:1], m[1:2]).astype(BF16)
    gb_ref[...] = _dot(h, wb_ref[...]).astype(gb_ref.dtype)
    p_ref[...] = (_dot(h, wc_ref[...]) * _dot(h, wv_ref[...])).astype(p_ref.dtype)


def _conv_out_kernel(gb_ref, p_ref, pp_ref, pn_ref, k_ref, cb_ref, w_ref, x_ref, m_ref, o_ref):
    i = pl.program_id(0)
    last = pl.num_programs(0) - 1
    p = p_ref[...].astype(F32)
    rows = lax.broadcasted_iota(jnp.int32, p.shape, 0)
    prev_row = jnp.where(i >= 2, pp_ref[7:8, :].astype(F32), 0.0)
    next_row = jnp.where(jnp.logical_and(i >= 1, i < last), pn_ref[0:1, :].astype(F32), 0.0)
    p_dn = jnp.where(rows == 0, prev_row, pltpu.roll(p, 1, 0))
    p_up = jnp.where(rows == TM - 1, next_row, pltpu.roll(p, TM - 1, 0))
    z = k_ref[0:1, :] * p_dn + k_ref[1:2, :] * p + k_ref[2:3, :] * p_up + cb_ref[...]
    q = (gb_ref[...].astype(F32) * z).astype(BF16)
    o_ref[...] = x_ref[...] + m_ref[0][2:3] * _dot(q, w_ref[...])


def _conv_layer(xa, mods, g1, w_in, conv_k, conv_b, w_out):
    t, d = xa.shape
    tn = 1024
    nt = d // tn
    w_in = w_in.astype(BF16)
    gb, p = pl.pallas_call(
        _conv_in_kernel,
        grid=(nt, t // TM),
        in_specs=[pl.BlockSpec((TM, d), lambda j, i: (i, 0)),
                  pl.BlockSpec((1, d), lambda j, i: (0, 0)),
                  pl.BlockSpec((1, 6, d), lambda j, i: (_mod_row(i), 0, 0)),
                  pl.BlockSpec((d, tn), lambda j, i: (0, j)),
                  pl.BlockSpec((d, tn), lambda j, i: (0, nt + j)),
                  pl.BlockSpec((d, tn), lambda j, i: (0, 2 * nt + j))],
        out_specs=[pl.BlockSpec((TM, tn), lambda j, i: (i, j))] * 2,
        out_shape=[jax.ShapeDtypeStruct((t, d), BF16)] * 2,
        compiler_params=_params(("parallel", "parallel")),
        name="conv_in",
    )(xa, g1.reshape(1, d), mods, w_in, w_in, w_in)
    r8 = TM // 8
    n8 = t // 8
    return pl.pallas_call(
        _conv_out_kernel,
        grid=(t // TM,),
        in_specs=[pl.BlockSpec((TM, d), lambda i: (i, 0)),
                  pl.BlockSpec((TM, d), lambda i: (i, 0)),
                  pl.BlockSpec((8, d), lambda i: (jnp.maximum(i * r8 - 1, 0), 0)),
                  pl.BlockSpec((8, d), lambda i: (jnp.minimum((i + 1) * r8, n8 - 1), 0)),
                  pl.BlockSpec((3, d), lambda i: (0, 0)),
                  pl.BlockSpec((1, d), lambda i: (0, 0)),
                  pl.BlockSpec((d, d), lambda i: (0, 0)),
                  pl.BlockSpec((TM, d), lambda i: (i, 0)),
                  pl.BlockSpec((1, 6, d), lambda i: (_mod_row(i), 0, 0))],
        out_specs=pl.BlockSpec((TM, d), lambda i: (i, 0)),
        out_shape=jax.ShapeDtypeStruct((t, d), F32),
        compiler_params=_params(("parallel",)),
        name="conv_out",
    )(gb, p, p, p, conv_k, conv_b.reshape(1, d), w_out.astype(BF16), xa, mods)


def _rope_tables(n_lat):
    rows = n_lat // GRID_W
    r = jnp.repeat(jnp.arange(rows, dtype=F32), GRID_W)
    col = jnp.tile(jnp.arange(GRID_W, dtype=F32), rows)
    inv = ROPE_THETA ** (-jnp.arange(ROPE_AXIS_PAIRS, dtype=F32) / ROPE_AXIS_PAIRS)
    ang = jnp.concatenate([r[:, None] * inv, col[:, None] * inv], axis=-1)
    cos, sin = jnp.cos(ang), jnp.sin(ang)
    cos_f = jnp.concatenate([cos, cos], axis=-1)
    sin_f = jnp.concatenate([-sin, sin], axis=-1)
    cos_f = jnp.concatenate([jnp.ones((CTX_LEN, HEAD_DIM), F32), cos_f], axis=0)
    sin_f = jnp.concatenate([jnp.zeros((CTX_LEN, HEAD_DIM), F32), sin_f], axis=0)
    return cos_f, sin_f


def _qkv_kernel(x_ref, g_ref, m_ref, w_ref, qg_ref, kg_ref, cos_ref, sin_ref, q_ref, k_ref, v_ref):
    m = m_ref[0]
    h = _modulate(x_ref[...], g_ref[...], m[0:1], m[1:2]).astype(BF16)
    a = _dot(h, w_ref[...])
    cos, sin = cos_ref[...], sin_ref[...]
    nq = N_HEADS * HEAD_DIM
    nk = N_KV_HEADS * HEAD_DIM

    def norm_rope(u, gain):
        ms = jnp.mean(u * u, axis=-1, keepdims=True)
        un = u * lax.rsqrt(ms + NORM_EPS) * gain
        return un * cos + pltpu.roll(un, HEAD_DIM // 2, 1) * sin

    for hh in range(N_HEADS):
        sl = slice(hh * HEAD_DIM, (hh + 1) * HEAD_DIM)
        q_ref[:, sl] = (norm_rope(a[:, sl], qg_ref[...]) * (HEAD_DIM ** -0.5 * math.log2(math.e))).astype(q_ref.dtype)
    for hh in range(N_KV_HEADS):
        sl = slice(hh * HEAD_DIM, (hh + 1) * HEAD_DIM)
        k_ref[:, sl] = norm_rope(a[:, nq + hh * HEAD_DIM:nq + (hh + 1) * HEAD_DIM], kg_ref[...]).astype(k_ref.dtype)
    v_ref[...] = a[:, nq + nk:].astype(v_ref.dtype)


def _attn_kernel(q_ref, k_ref, v_ref, o_ref, s_ref, m_ref, l_ref, acc_ref):
    i = pl.program_id(1)
    t = k_ref.shape[0]
    nt_dims = (((1,), (1,)), ((), ()))
    m_ref[...] = jnp.full(m_ref.shape, -jnp.inf, F32)
    l_ref[...] = jnp.zeros(l_ref.shape, F32)
    acc_ref[...] = jnp.zeros(acc_ref.shape, F32)

    def scores(slot, start, size):
        kc = k_ref[pl.ds(start, size), :]
        for j in range(Q_PER_KV):
            s_ref[slot, j, :, 0:size] = lax.dot_general(q_ref[:, j * HEAD_DIM:(j + 1) * HEAD_DIM], kc, nt_dims,
                                                        preferred_element_type=F32)

    def update(slot, start, size):
        vc = v_ref[pl.ds(start, size), :]
        for j in range(Q_PER_KV):
            s = s_ref[slot, j, :, 0:size]
            m_old = m_ref[j]
            m_new = jnp.maximum(m_old, jnp.max(s, axis=-1, keepdims=True))
            alpha = jnp.exp2(m_old - m_new)
            p = jnp.exp2(s - jnp.tile(m_new, (1, size // LANES)))
            l_ref[j] = alpha * l_ref[j] + jnp.sum(p, axis=-1, keepdims=True)
            acc_ref[j] = alpha * acc_ref[j] + _dot(p.astype(BF16), vc)
            m_ref[j] = m_new

    scores(0, 0, CTX_LEN)
    update(0, 0, CTX_LEN)

    @pl.when(i > 0)
    def _():
        n_pairs = (t - CTX_LEN) // (2 * ATT_TK)
        scores(0, CTX_LEN, ATT_TK)

        def body(c, carry):
            base = pl.multiple_of(CTX_LEN + 2 * c * ATT_TK, CTX_LEN)
            scores(1, base + ATT_TK, ATT_TK)
            update(0, base, ATT_TK)
            nxt = pl.multiple_of(jnp.minimum(base + 2 * ATT_TK, t - ATT_TK), CTX_LEN)
            scores(0, nxt, ATT_TK)
            update(1, base + ATT_TK, ATT_TK)
            return carry

        lax.fori_loop(0, n_pairs, body, 0)

    for j in range(Q_PER_KV):
        o_ref[:, j * HEAD_DIM:(j + 1) * HEAD_DIM] = (acc_ref[j] / l_ref[j]).astype(o_ref.dtype)


def _attn_layer(xa, mods, g1, wq, wk, wv, wo, q_g, k_g):
    t, d = xa.shape
    nq, nk = N_HEADS * HEAD_DIM, N_KV_HEADS * HEAD_DIM
    w = jnp.concatenate([wq, wk, wv], axis=1).astype(BF16)
    cos_f, sin_f = _rope_tables(t - CTX_LEN)
    q, k, v = pl.pallas_call(
        _qkv_kernel,
        grid=(t // TM,),
        in_specs=[pl.BlockSpec((TM, d), lambda i: (i, 0)),
                  pl.BlockSpec((1, d), lambda i: (0, 0)),
                  pl.BlockSpec((1, 6, d), lambda i: (_mod_row(i), 0, 0)),
                  pl.BlockSpec((d, nq + 2 * nk), lambda i: (0, 0)),
                  pl.BlockSpec((1, HEAD_DIM), lambda i: (0, 0)),
                  pl.BlockSpec((1, HEAD_DIM), lambda i: (0, 0)),
                  pl.BlockSpec((TM, HEAD_DIM), lambda i: (i, 0)),
                  pl.BlockSpec((TM, HEAD_DIM), lambda i: (i, 0))],
        out_specs=[pl.BlockSpec((TM, nq), lambda i: (i, 0)),
                   pl.BlockSpec((TM, nk), lambda i: (i, 0)),
                   pl.BlockSpec((TM, nk), lambda i: (i, 0))],
        out_shape=[jax.ShapeDtypeStruct((t, nq), BF16),
                   jax.ShapeDtypeStruct((t, nk), BF16),
                   jax.ShapeDtypeStruct((t, nk), BF16)],
        compiler_params=_params(("parallel",)),
        name="qkv",
    )(xa, g1.reshape(1, d), mods, w, q_g.reshape(1, HEAD_DIM), k_g.reshape(1, HEAD_DIM), cos_f, sin_f)
    qw = Q_PER_KV * HEAD_DIM
    o = pl.pallas_call(
        _attn_kernel,
        grid=(N_KV_HEADS, t // TM),
        in_specs=[pl.BlockSpec((TM, qw), lambda n, i: (i, n)),
                  pl.BlockSpec((t, HEAD_DIM), lambda n, i: (0, n)),
                  pl.BlockSpec((t, HEAD_DIM), lambda n, i: (0, n))],
        out_specs=pl.BlockSpec((TM, qw), lambda n, i: (i, n)),
        out_shape=jax.ShapeDtypeStruct((t, nq), BF16),
        scratch_shapes=[pltpu.VMEM((2, Q_PER_KV, TM, ATT_TK), F32),
                        pltpu.VMEM((Q_PER_KV, TM, LANES), F32),
                        pltpu.VMEM((Q_PER_KV, TM, LANES), F32),
                        pltpu.VMEM((Q_PER_KV, TM, HEAD_DIM), F32)],
        compiler_params=_params(("parallel", "parallel")),
        name="attention",
    )(q, k, v)
    return _mm_resid(o, wo.astype(BF16), jnp.zeros((d,), F32), xa, mods, 2)


def _s5_matrices(a_re, a_im, log_dt, b_re, b_im, c_re, c_im, reverse):
    l, g, p, c = SSM_SUB, SSM_GROUPS, SSM_STATE, SSM_GROUP_DIM
    a = lax.complex(a_re.astype(F32), a_im.astype(F32))
    adt = a * jnp.exp(log_dt.astype(F32))[:, None]
    a_bar = jnp.exp(adt)
    b_bar = ((a_bar - 1) / a)[..., None] * lax.complex(b_re.astype(F32), b_im.astype(F32))
    c_mat = lax.complex(c_re.astype(F32), c_im.astype(F32))
    pw = jnp.exp(adt[:, None, :] * jnp.arange(l + 1, dtype=F32)[None, :, None])
    kern = jnp.einsum('gcp,gkp,gpd->gkcd', c_mat, pw[:, :l], b_bar,
                      precision=lax.Precision.HIGHEST).real
    lag = jnp.arange(l)[None, :] - jnp.arange(l)[:, None]
    m = jnp.where((lag >= 0)[None, :, :, None, None], kern[:, jnp.clip(lag, 0, l - 1)], 0.0)
    m = jnp.transpose(m, (0, 1, 4, 2, 3))
    e = pw[:, l - 1 - jnp.arange(l), :][:, :, None, :] * jnp.transpose(b_bar, (0, 2, 1))[:, None, :, :]
    cp = c_mat[:, None, :, :] * pw[:, 1:, None, :]
    cp = jnp.transpose(cp, (0, 3, 1, 2))
    if reverse:
        m = jnp.flip(m, (1, 3))
        e = jnp.flip(e, 1)
        cp = jnp.flip(cp, 2)
    eye = jnp.eye(SSM_GPB, dtype=F32)

    def blockdiag(x, pattern):
        return jnp.einsum(pattern, x.reshape((SSM_CB, SSM_GPB) + x.shape[1:]), eye)

    mbig = blockdiag(m, 'bisctd,ij->bsictjd').reshape(SSM_CB, l * LANES, l * LANES)
    e_re = blockdiag(e.real, 'biscp,ij->bsicjp').reshape(SSM_CB, l * LANES, SSM_GPB * p)
    e_im = blockdiag(e.imag, 'biscp,ij->bsicjp').reshape(SSM_CB, l * LANES, SSM_GPB * p)
    cp_re = blockdiag(cp.real, 'biptd,ij->biptjd').reshape(SSM_CB, SSM_GPB * p, l * LANES)
    cp_im = blockdiag(-cp.imag, 'biptd,ij->biptjd').reshape(SSM_CB, SSM_GPB * p, l * LANES)
    w1 = jnp.concatenate([e_re, e_im], axis=2).astype(BF16)
    w2 = jnp.concatenate([mbig, cp_re, cp_im], axis=1).astype(BF16)
    al = pw[:, l].reshape(SSM_CB, 1, SSM_GPB * p)
    return w1, w2, al.real, al.imag


def _s5_scan_kernel(*refs, reverse):
    l = SSM_SUB
    u_refs = refs[:l]
    w1_ref, w2_ref, ar_ref, ai_ref, h0_ref, y_ref, hfin_ref, v_ref, h_ref = refs[l:]
    ns = ar_ref.shape[-1]
    rows = v_ref.shape[0]

    @pl.when(pl.program_id(1) == 0)
    def _():
        h_ref[...] = h0_ref[0]

    u = jnp.concatenate([r[...] for r in u_refs], axis=1)
    v_ref[...] = _dot(u, w1_ref[0])
    ar, ai = ar_ref[0], ai_ref[0]

    def body(step, carry):
        hr, hi = carry
        j = rows - 1 - step if reverse else step
        vr = v_ref[pl.ds(j, 1), 0:ns]
        vi = v_ref[pl.ds(j, 1), ns:2 * ns]
        v_ref[pl.ds(j, 1), 0:ns] = hr
        v_ref[pl.ds(j, 1), ns:2 * ns] = hi
        return ar * hr - ai * hi + vr, ar * hi + ai * hr + vi

    hr, hi = lax.fori_loop(0, rows, body, (h_ref[0:1, :], h_ref[1:2, :]))
    h_ref[0:1, :] = hr
    h_ref[1:2, :] = hi
    hfin_ref[0] = h_ref[...]
    y = _dot(jnp.concatenate([u, v_ref[...].astype(BF16)], axis=1), w2_ref[0])
    for t in range(l):
        y_ref[t] = y[:, t * LANES:(t + 1) * LANES]


def _s5_scan(seq, mats, h0, reverse):
    t, d = seq.shape
    l = SSM_SUB
    w1, w2, al_re, al_im = mats
    n_sub = t // l
    rows = max(r for r in range(8, min(n_sub, 512) + 1, 8) if n_sub % r == 0)
    nr = n_sub // rows
    ns = SSM_GPB * SSM_STATE
    u2 = seq.reshape(n_sub, l * d)
    row_blk = (lambda r: nr - 1 - r) if reverse else (lambda r: r)
    blk = lambda s: pl.BlockSpec((rows, LANES), lambda cb, r, s=s: (row_blk(r), s * SSM_CB + cb))
    return pl.pallas_call(
        functools.partial(_s5_scan_kernel, reverse=reverse),
        grid=(SSM_CB, nr),
        in_specs=[blk(s) for s in range(l)] + [
            pl.BlockSpec((1,) + w1.shape[1:], lambda cb, r: (cb, 0, 0)),
            pl.BlockSpec((1,) + w2.shape[1:], lambda cb, r: (cb, 0, 0)),
            pl.BlockSpec((1, 1, ns), lambda cb, r: (cb, 0, 0)),
            pl.BlockSpec((1, 1, ns), lambda cb, r: (cb, 0, 0)),
            pl.BlockSpec((1, 2, ns), lambda cb, r: (cb, 0, 0))],
        out_specs=[pl.BlockSpec((l, rows, LANES), lambda cb, r: (0, row_blk(r), cb)),
                   pl.BlockSpec((1, 2, ns), lambda cb, r: (cb, 0, 0))],
        out_shape=[jax.ShapeDtypeStruct((l, n_sub, d), F32),
                   jax.ShapeDtypeStruct((SSM_CB, 2, ns), F32)],
        scratch_shapes=[pltpu.VMEM((rows, 2 * ns), F32), pltpu.VMEM((2, ns), F32)],
        compiler_params=_params(("parallel", "arbitrary")),
        name="s5_scan",
    )(*([u2] * l), w1, w2, al_re, al_im, h0)


def _gelu_tanh(x):
    return 0.5 * x * (1.0 + jnp.tanh(math.sqrt(2.0 / math.pi) * (x + 0.044715 * x * x * x)))


def _s5_glu_kernel(x_ref, g_ref, m_ref, d_ref, yf_ref, yb_ref, wa_ref, wg_ref, ba_ref, bg_ref, xo_ref, mo_ref, o_ref):
    m = m_ref[0]
    h = _modulate(x_ref[...], g_ref[...], m[0:1], m[1:2])
    z = _gelu_tanh(d_ref[...] * h + yf_ref[0] + yb_ref[0]).astype(BF16)
    a = _dot(z, wa_ref[...]) + ba_ref[...]
    g = _dot(z, wg_ref[...]) + bg_ref[...]
    o_ref[...] = xo_ref[...] + mo_ref[0][2:3] * (a * jax.nn.sigmoid(g))


def _s5_layer(xa, mods, g1, a_re, a_im, log_dt, b_re, b_im, c_re, c_im, d_skip, glu_w, glu_b):
    t, d = xa.shape
    l = SSM_SUB
    h = _prep(xa, g1, mods, 0)
    h_ctx, h_lat = h[:CTX_LEN], h[CTX_LEN:]
    x_lat = xa[CTX_LEN:]
    ns = SSM_GPB * SSM_STATE
    ys = []
    for direction in range(2):
        rev = direction == 1
        mats = _s5_matrices(a_re[direction], a_im[direction], log_dt[direction], b_re[direction], b_im[direction],
                            c_re[direction], c_im[direction], rev)
        _, state = _s5_scan(h_ctx, mats, jnp.zeros((SSM_CB, 2, ns), F32), rev)
        y, _ = _s5_scan(h_lat, mats, state, rev)
        ys.append(y)
    tn = 1024
    nt = d // tn
    n_sub = (t - CTX_LEN) // l
    assert n_sub % TM == 0
    glu_w = glu_w.astype(BF16)
    out = pl.pallas_call(
        _s5_glu_kernel,
        grid=(nt, n_sub // TM, l),
        in_specs=[pl.BlockSpec((TM, d), lambda j, i, s: (i, s)),
                  pl.BlockSpec((1, d), lambda j, i, s: (0, 0)),
                  pl.BlockSpec((1, 6, d), lambda j, i, s: (1, 0, 0)),
                  pl.BlockSpec((1, d), lambda j, i, s: (0, 0)),
                  pl.BlockSpec((1, TM, d), lambda j, i, s: (s, i, 0)),
                  pl.BlockSpec((1, TM, d), lambda j, i, s: (s, i, 0)),
                  pl.BlockSpec((d, tn), lambda j, i, s: (0, j)),
                  pl.BlockSpec((d, tn), lambda j, i, s: (0, nt + j)),
                  pl.BlockSpec((1, tn), lambda j, i, s: (0, j)),
                  pl.BlockSpec((1, tn), lambda j, i, s: (0, nt + j)),
                  pl.BlockSpec((TM, tn), lambda j, i, s: (i, s * nt + j)),
                  pl.BlockSpec((1, 6, tn), lambda j, i, s: (1, 0, j))],
        out_specs=pl.BlockSpec((TM, tn), lambda j, i, s: (i, s * nt + j)),
        out_shape=jax.ShapeDtypeStruct((n_sub, l * d), F32),
        compiler_params=_params(("parallel", "parallel", "parallel")),
        name="s5_glu",
    )(x_lat.reshape(n_sub, l * d), g1.reshape(1, d), mods, d_skip.reshape(1, d), ys[0], ys[1], glu_w, glu_w,
      glu_b.reshape(1, 2 * d), glu_b.reshape(1, 2 * d), x_lat.reshape(n_sub, l * d), mods)
    return out.reshape(t - CTX_LEN, d)


def _router_kernel(x_ref, g_ref, m_ref, wr_ref, br_ref, tri_ref, h_ref, idx_ref, gate_ref, rank_ref, cnt_ref, base_ref):
    i = pl.program_id(0)

    @pl.when(i == 0)
    def _():
        base_ref[...] = jnp.zeros(base_ref.shape, F32)

    m = m_ref[0]
    h = _modulate(x_ref[...], g_ref[...], m[3:4], m[4:5])
    h_ref[...] = h.astype(h_ref.dtype)
    logits = lax.dot_general(wr_ref[...], h, (((1,), (1,)), ((), ())), precision=lax.Precision.HIGHEST,
                             preferred_element_type=F32) + br_ref[...]
    e_iota = lax.broadcasted_iota(jnp.int32, logits.shape, 0).astype(F32)
    base = base_ref[...]
    tops, idxs, ranks = [], [], []
    for _ in range(TOP_K):
        top = jnp.max(logits, axis=0, keepdims=True)
        idx = jnp.min(jnp.where(logits == top, e_iota, float(N_EXPERTS)), axis=0, keepdims=True)
        sel = e_iota == idx
        logits = jnp.where(sel, -jnp.inf, logits)
        onehot = sel.astype(BF16)
        before = _dot(onehot, tri_ref[...])
        ranks.append(jnp.sum(jnp.where(sel, base + before, 0.0), axis=0, keepdims=True))
        base = base + jnp.sum(sel.astype(F32), axis=1, keepdims=True)
        tops.append(top)
        idxs.append(idx)
    base_ref[...] = base
    cnt_ref[...] = jnp.broadcast_to(base, cnt_ref.shape)
    ex = [jnp.exp(v - tops[0]) for v in tops]
    den = ex[0] + ex[1] + ex[2] + ex[3]
    idx_ref[...] = jnp.concatenate(idxs, axis=0).astype(jnp.int32)
    gate_ref[...] = jnp.concatenate([v / den for v in ex], axis=0)
    rank_ref[...] = jnp.concatenate(ranks, axis=0).astype(jnp.int32)


def _ffn_kernel(be_ref, x_ref, w1_ref, b1_ref, w2_ref, b2_ref, o_ref, w1b_ref, w2b_ref):
    i = pl.program_id(0)
    new_expert = jnp.logical_or(i == 0, be_ref[0, i] != be_ref[0, jnp.maximum(i - 1, 0)])

    @pl.when(new_expert)
    def _():
        def cast(ref, dst):
            n_chunks = ref.shape[1] // WCAST_ROWS

            def body(r, carry):
                rows = pl.ds(pl.multiple_of(r * WCAST_ROWS, WCAST_ROWS), WCAST_ROWS)
                dst[rows, :] = ref[0, rows, :].astype(BF16)
                return carry

            lax.fori_loop(0, n_chunks, body, 0)

        cast(w1_ref, w1b_ref)
        cast(w2_ref, w2b_ref)

    @pl.when(be_ref[1, i] > 0)
    def _():
        h = _dot(x_ref[...], w1b_ref[...]) + b1_ref[0]
        gate = jnp.minimum(h[:, :D_EXPERT], SWIGLU_LIMIT)
        lin = jnp.clip(h[:, D_EXPERT:], -SWIGLU_LIMIT, SWIGLU_LIMIT)
        act = ((lin + 1.0) * gate * jax.nn.sigmoid(SWIGLU_ALPHA * gate)).astype(BF16)
        o_ref[...] = (_dot(act, w2b_ref[...]) + b2_ref[0]).astype(o_ref.dtype)


def _combine_kernel(y_ref, gt_ref, x_ref, m_ref, o_ref):
    gt = gt_ref[...]
    acc = gt[:, 0:1] * y_ref[0].astype(F32)
    for k in range(1, TOP_K):
        acc = acc + gt[:, k:k + 1] * y_ref[k].astype(F32)
    o_ref[...] = x_ref[...] + m_ref[0][5:6] * acc


def _moe_layer(xa, mods, g2, w_r, b_r, w1, b1, w2, b2):
    t, d = xa.shape
    nb = t // TM
    tri = jnp.asarray(np.triu(np.ones((TM, TM), np.float32), 1), BF16)
    h2, idx, gates, rank, cnt = pl.pallas_call(
        _router_kernel,
        grid=(nb,),
        in_specs=[pl.BlockSpec((TM, d), lambda i: (i, 0)),
                  pl.BlockSpec((1, d), lambda i: (0, 0)),
                  pl.BlockSpec((1, 6, d), lambda i: (_mod_row(i), 0, 0)),
                  pl.BlockSpec((N_EXPERTS, d), lambda i: (0, 0)),
                  pl.BlockSpec((N_EXPERTS, 1), lambda i: (0, 0)),
                  pl.BlockSpec((TM, TM), lambda i: (0, 0))],
        out_specs=[pl.BlockSpec((TM, d), lambda i: (i, 0)),
                   pl.BlockSpec((TOP_K, TM), lambda i: (0, i)),
                   pl.BlockSpec((TOP_K, TM), lambda i: (0, i)),
                   pl.BlockSpec((TOP_K, TM), lambda i: (0, i)),
                   pl.BlockSpec((N_EXPERTS, LANES), lambda i: (0, 0))],
        out_shape=[jax.ShapeDtypeStruct((t, d), BF16),
                   jax.ShapeDtypeStruct((TOP_K, t), jnp.int32),
                   jax.ShapeDtypeStruct((TOP_K, t), F32),
                   jax.ShapeDtypeStruct((TOP_K, t), jnp.int32),
                   jax.ShapeDtypeStruct((N_EXPERTS, LANES), F32)],
        scratch_shapes=[pltpu.VMEM((N_EXPERTS, 1), F32)],
        compiler_params=_params(("arbitrary",)),
        name="router",
    )(xa, g2.reshape(1, d), mods, w_r.T, b_r.reshape(N_EXPERTS, 1), tri)

    counts = cnt[:, 0].astype(jnp.int32)
    padded = (counts + MOE_TM - 1) // MOE_TM * MOE_TM
    pend = jnp.cumsum(padded)
    pstart = pend - padded
    dest = pstart[idx] + rank
    n_rows = -(-t * TOP_K // MOE_TM) * MOE_TM + N_EXPERTS * MOE_TM
    n_blocks = n_rows // MOE_TM
    blk_start = jnp.arange(n_blocks, dtype=jnp.int32) * MOE_TM
    blk_used = (blk_start < pend[-1]).astype(jnp.int32)
    blk_e = jnp.minimum(jnp.searchsorted(pend, blk_start, side='right'), N_EXPERTS - 1).astype(jnp.int32)
    last_e = blk_e[jnp.maximum(pend[-1] // MOE_TM - 1, 0)]
    blk_e = jnp.where(blk_used > 0, blk_e, last_e)
    be = jnp.stack([blk_e, blk_used])
    tok = jnp.broadcast_to(jnp.arange(t, dtype=jnp.int32)[None, :], (TOP_K, t))
    row_tok = jnp.zeros((n_rows,), jnp.int32).at[dest.reshape(-1)].set(tok.reshape(-1))
    xs = jnp.take(h2, row_tok, axis=0)

    ys = pl.pallas_call(
        _ffn_kernel,
        grid_spec=pltpu.PrefetchScalarGridSpec(
            num_scalar_prefetch=1,
            grid=(n_blocks,),
            in_specs=[pl.BlockSpec((MOE_TM, d), lambda i, be: (i, 0)),
                      pl.BlockSpec((1, d, 2 * D_EXPERT), lambda i, be: (be[0, i], 0, 0),
                                   pipeline_mode=pl.Buffered(1)),
                      pl.BlockSpec((1, 1, 2 * D_EXPERT), lambda i, be: (be[0, i], 0, 0)),
                      pl.BlockSpec((1, D_EXPERT, d), lambda i, be: (be[0, i], 0, 0),
                                   pipeline_mode=pl.Buffered(1)),
                      pl.BlockSpec((1, 1, d), lambda i, be: (be[0, i], 0, 0))],
            out_specs=pl.BlockSpec((MOE_TM, d), lambda i, be: (i, 0)),
            scratch_shapes=[pltpu.VMEM((d, 2 * D_EXPERT), BF16), pltpu.VMEM((D_EXPERT, d), BF16)]),
        out_shape=jax.ShapeDtypeStruct((n_rows, d), BF16),
        compiler_params=_params(("arbitrary",)),
        name="moe_ffn",
    )(be, xs, w1, b1.reshape(N_EXPERTS, 1, 2 * D_EXPERT), w2, b2.reshape(N_EXPERTS, 1, d))

    y4 = jnp.take(ys, dest.reshape(-1), axis=0).reshape(TOP_K, t, d)
    return pl.pallas_call(
        _combine_kernel,
        grid=(nb,),
        in_specs=[pl.BlockSpec((TOP_K, TM, d), lambda i: (0, i, 0)),
                  pl.BlockSpec((TM, TOP_K), lambda i: (i, 0)),
                  pl.BlockSpec((TM, d), lambda i: (i, 0)),
                  pl.BlockSpec((1, 6, d), lambda i: (_mod_row(i), 0, 0))],
        out_specs=pl.BlockSpec((TM, d), lambda i: (i, 0)),
        out_shape=jax.ShapeDtypeStruct((t, d), F32),
        compiler_params=_params(("parallel",)),
        name="moe_combine",
    )(y4, gates.T, xa, mods)


def kernel(x, c, ctx, c_ctx, ada_w, ada_b, norm1_g, norm2_g, fourier_w_out, fourier_b_out, conv_w_in, conv_k, conv_b, conv_w_out, attn_wq, attn_wk, attn_wv, attn_wo, attn_q_g, attn_k_g, ssm_a_re, ssm_a_im, ssm_log_dt, ssm_b_re, ssm_b_im, ssm_c_re, ssm_c_im, ssm_d, ssm_glu_w, ssm_glu_b, router_w, router_b, moe_w1, moe_b1, moe_w2, moe_b2):
    bsz, n, d = x.shape
    assert bsz == 1 and d == D_MODEL and ctx.shape[1] == CTX_LEN and n % TM == 0
    depth = ada_w.shape[0]
    cond8 = jnp.zeros((8, d), F32).at[0].set(c_ctx).at[1].set(c[0])
    mods = _ada_all(cond8, ada_w, ada_b).reshape(depth, 8, 6, d)
    assert depth == 4
    xa = jnp.concatenate([ctx[0], x[0]], axis=0)
    for i in range(depth):
        m, j = i % 4, i // 4
        mods_i = mods[i]
        if m == 0:
            xa = _fourier_layer(xa, mods[i], norm1_g[i], fourier_w_out[j], fourier_b_out[j])
        elif m == 1:
            xa = _conv_layer(xa, mods[i], norm1_g[i], conv_w_in[j], conv_k[j], conv_b[j], conv_w_out[j])
        elif m == 2:
            xa = _attn_layer(xa, mods[i], norm1_g[i], attn_wq[j], attn_wk[j], attn_wv[j], attn_wo[j],
                             attn_q_g[j], attn_k_g[j])
        else:
            xa = _s5_layer(xa, mods[i], norm1_g[i], ssm_a_re[j], ssm_a_im[j], ssm_log_dt[j], ssm_b_re[j],
                           ssm_b_im[j], ssm_c_re[j], ssm_c_im[j], ssm_d[j], ssm_glu_w[j], ssm_glu_b[j])
            mods_i = mods_i.at[0].set(mods_i[1])
        xa = _moe_layer(xa, mods_i, norm2_g[i], router_w[i], router_b[i], moe_w1[i], moe_b1[i], moe_w2[i], moe_b2[i])
    return xa[None]
```

```python
import functools
import math

import numpy as np
import jax
import jax.numpy as jnp
from jax import lax
from jax.experimental import pallas as pl
from jax.experimental.pallas import tpu as pltpu

F32 = jnp.float32
BF16 = jnp.bfloat16

D_MODEL = 2048
GRID_W = 64
CTX_LEN = 256
FOURIER_GROUPS = 8
FOURIER_GROUP_DIM = D_MODEL // FOURIER_GROUPS
HEAD_DIM = 128
N_HEADS = D_MODEL // HEAD_DIM
N_KV_HEADS = 4
Q_PER_KV = N_HEADS // N_KV_HEADS
ROPE_AXIS_PAIRS = HEAD_DIM // 4
ROPE_THETA = 10000.0
SSM_GROUP_DIM = 16
SSM_GROUPS = D_MODEL // SSM_GROUP_DIM
SSM_STATE = 64
N_EXPERTS = 32
TOP_K = 4
D_EXPERT = 1024
SWIGLU_LIMIT = 7.0
SWIGLU_ALPHA = 1.702
NORM_EPS = 1e-6

LANES = 128
TM = 256
DFT_N2 = 128
SSM_SUB = 8
SSM_CB = D_MODEL // LANES
SSM_GPB = LANES // SSM_GROUP_DIM
MOE_TM = 512
ATT_TK = 512
WCAST_ROWS = 256
VMEM_LIMIT = 56 * 2 ** 20


def _params(sem, vmem=VMEM_LIMIT):
    return pltpu.CompilerParams(dimension_semantics=sem, vmem_limit_bytes=vmem)


def _mod_row(i):
    return jnp.minimum(i, 1)


def _modulate(x, g, shift, scale):
    ms = jnp.mean(x * x, axis=-1, keepdims=True)
    return (x * lax.rsqrt(ms + NORM_EPS) * g) * (1.0 + scale) + shift


def _dot(a, b):
    return jnp.dot(a, b, preferred_element_type=F32)


def _ada_kernel(c_ref, w_ref, b_ref, o_ref):
    c = c_ref[...]
    s = c * jax.nn.sigmoid(c)
    o_ref[0] = jnp.dot(s, w_ref[0], precision=lax.Precision.HIGHEST, preferred_element_type=F32) + b_ref[0]


def _ada_all(cond8, ada_w, ada_b):
    depth, d, n6 = ada_w.shape
    tn = 1024
    return pl.pallas_call(
        _ada_kernel,
        grid=(depth, n6 // tn),
        in_specs=[pl.BlockSpec((8, d), lambda l, j: (0, 0)),
                  pl.BlockSpec((1, d, tn), lambda l, j: (l, 0, j)),
                  pl.BlockSpec((1, 1, tn), lambda l, j: (l, 0, j))],
        out_specs=pl.BlockSpec((1, 8, tn), lambda l, j: (l, 0, j)),
        out_shape=jax.ShapeDtypeStruct((depth, 8, n6), F32),
        compiler_params=_params(("parallel", "parallel")),
        name="ada",
    )(cond8, ada_w, ada_b.reshape(depth, 1, n6))


def _prep_kernel(x_ref, g_ref, m_ref, o_ref, *, which):
    m = m_ref[0]
    h = _modulate(x_ref[...], g_ref[...], m[3 * which:3 * which + 1], m[3 * which + 1:3 * which + 2])
    o_ref[...] = h.astype(o_ref.dtype)


def _prep(xa, g, mods, which):
    t, d = xa.shape
    return pl.pallas_call(
        functools.partial(_prep_kernel, which=which),
        grid=(t // TM,),
        in_specs=[pl.BlockSpec((TM, d), lambda i: (i, 0)),
                  pl.BlockSpec((1, d), lambda i: (0, 0)),
                  pl.BlockSpec((1, 6, d), lambda i: (_mod_row(i), 0, 0))],
        out_specs=pl.BlockSpec((TM, d), lambda i: (i, 0)),
        out_shape=jax.ShapeDtypeStruct((t, d), BF16),
        compiler_params=_params(("parallel",)),
        name="prep",
    )(xa, g.reshape(1, d), mods)


def _mm_resid_kernel(a_ref, w_ref, b_ref, x_ref, m_ref, o_ref, *, gate_row):
    y = _dot(a_ref[...], w_ref[...]) + b_ref[...]
    o_ref[...] = x_ref[...] + m_ref[0][gate_row:gate_row + 1] * y


def _mm_resid(a, w, b, xa, mods, gate_row):
    t, d = xa.shape
    k = a.shape[1]
    return pl.pallas_call(
        functools.partial(_mm_resid_kernel, gate_row=gate_row),
        grid=(t // TM,),
        in_specs=[pl.BlockSpec((TM, k), lambda i: (i, 0)),
                  pl.BlockSpec((k, d), lambda i: (0, 0)),
                  pl.BlockSpec((1, d), lambda i: (0, 0)),
                  pl.BlockSpec((TM, d), lambda i: (i, 0)),
                  pl.BlockSpec((1, 6, d), lambda i: (_mod_row(i), 0, 0))],
        out_specs=pl.BlockSpec((TM, d), lambda i: (i, 0)),
        out_shape=jax.ShapeDtypeStruct((t, d), F32),
        compiler_params=_params(("parallel",)),
        name="mm_resid",
    )(a, w, b.reshape(1, d), xa, mods)


def _dft_tables(n1, n2):
    n = n1 * n2
    k1 = np.arange(n1, dtype=np.float64)
    t = (np.arange(n1, dtype=np.float64)[None, :] * n2 + np.arange(n2, dtype=np.float64)[:, None])
    ang = 2.0 * np.pi * k1[None, :, None] * t[:, None, :] / n
    g = np.concatenate([np.cos(ang), -np.sin(ang)], axis=1) / math.sqrt(n1)
    a2 = 2.0 * np.pi * np.outer(np.arange(n2), np.arange(n2)) / n2
    c2, s2 = np.cos(a2) / math.sqrt(n2), np.sin(a2) / math.sqrt(n2)
    f2 = np.block([[c2, s2], [-s2, c2]])
    return jnp.asarray(g, BF16), jnp.asarray(f2, BF16)


def _dft_stage1_kernel(x_ref, g_ref, re_ref, im_ref):
    n1 = x_ref.shape[0]
    a = _dot(g_ref[0], x_ref[...])
    re_ref[0] = a[:n1].astype(re_ref.dtype)
    im_ref[0] = a[n1:].astype(im_ref.dtype)


def _dft_stage2_kernel(re_ref, im_ref, f_ref, zre_ref, zim_ref):
    n2 = re_ref.shape[0]
    z = _dot(f_ref[...], jnp.concatenate([re_ref[...], im_ref[...]], axis=0))
    zre_ref[...] = z[:n2].astype(zre_ref.dtype)
    zim_ref[...] = z[n2:].astype(zim_ref.dtype)


def _pos_dft(h):
    n, d = h.shape
    if n <= 512:
        n1, n2 = n, 1
    else:
        n1, n2 = n // DFT_N2, DFT_N2
    g, f2 = _dft_tables(n1, n2)
    are, aim = pl.pallas_call(
        _dft_stage1_kernel,
        grid=(n2,),
        in_specs=[pl.BlockSpec((n1, d), lambda j: (0, j)),
                  pl.BlockSpec((1, 2 * n1, n1), lambda j: (j, 0, 0))],
        out_specs=[pl.BlockSpec((1, n1, d), lambda j: (j, 0, 0))] * 2,
        out_shape=[jax.ShapeDtypeStruct((n2, n1, d), BF16)] * 2,
        compiler_params=_params(("parallel",)),
        name="dft_stage1",
    )(h.reshape(n1, n2 * d), g)
    if n2 == 1:
        return are.reshape(n, d), aim.reshape(n, d)
    tn = 2048
    zre, zim = pl.pallas_call(
        _dft_stage2_kernel,
        grid=(n1 * d // tn,),
        in_specs=[pl.BlockSpec((n2, tn), lambda j: (0, j)),
                  pl.BlockSpec((n2, tn), lambda j: (0, j)),
                  pl.BlockSpec((2 * n2, 2 * n2), lambda j: (0, 0))],
        out_specs=[pl.BlockSpec((n2, tn), lambda j: (0, j))] * 2,
        out_shape=[jax.ShapeDtypeStruct((n2, n1 * d), BF16)] * 2,
        compiler_params=_params(("parallel",)),
        name="dft_stage2",
    )(are.reshape(n2, n1 * d), aim.reshape(n2, n1 * d), f2)
    return zre.reshape(n, d), zim.reshape(n, d)


def _fourier_out_kernel(zrc_ref, zic_ref, zrl_ref, zil_ref, cc_ref, sc_ref, w_ref, b_ref, x_ref, m_ref, o_ref):
    is_ctx = pl.program_id(0) == 0
    zre = jnp.where(is_ctx, zrc_ref[...], zrl_ref[...])
    zim = jnp.where(is_ctx, zic_ref[...], zil_ref[...])
    gd = FOURIER_GROUP_DIM
    parts = []
    for g in range(FOURIER_GROUPS):
        sl = slice(g * gd, (g + 1) * gd)
        parts.append(_dot(zre[:, sl], cc_ref[...]) + _dot(zim[:, sl], sc_ref[...]))
    f = jnp.concatenate(parts, axis=1).astype(BF16)
    y = _dot(f, w_ref[...]) + b_ref[...]
    o_ref[...] = x_ref[...] + m_ref[0][2:3] * y


def _fourier_layer(xa, mods, g1, w_out, b_out):
    t, d = xa.shape
    h = _prep(xa, g1, mods, 0)
    zrc, zic = _pos_dft(h[:CTX_LEN])
    zrl, zil = _pos_dft(h[CTX_LEN:])
    gd = FOURIER_GROUP_DIM
    ang = 2.0 * np.pi * np.outer(np.arange(gd), np.arange(gd)) / gd
    cc = jnp.asarray(np.cos(ang) / math.sqrt(gd), BF16)
    sc = jnp.asarray(np.sin(ang) / math.sqrt(gd), BF16)
    lat = lambda i: (jnp.maximum(i - 1, 0), 0)
    return pl.pallas_call(
        _fourier_out_kernel,
        grid=(t // TM,),
        in_specs=[pl.BlockSpec((TM, d), lambda i: (0, 0)),
                  pl.BlockSpec((TM, d), lambda i: (0, 0)),
                  pl.BlockSpec((TM, d), lat),
                  pl.BlockSpec((TM, d), lat),
                  pl.BlockSpec((gd, gd), lambda i: (0, 0)),
                  pl.BlockSpec((gd, gd), lambda i: (0, 0)),
                  pl.BlockSpec((d, d), lambda i: (0, 0)),
                  pl.BlockSpec((1, d), lambda i: (0, 0)),
                  pl.BlockSpec((TM, d), lambda i: (i, 0)),
                  pl.BlockSpec((1, 6, d), lambda i: (_mod_row(i), 0, 0))],
        out_specs=pl.BlockSpec((TM, d), lambda i: (i, 0)),
        out_shape=jax.ShapeDtypeStruct((t, d), F32),
        compiler_params=_params(("parallel",)),
        name="fourier_out",
    )(zrc, zic, zrl, zil, cc, sc, w_out.astype(BF16), b_out.reshape(1, d), xa, mods)


def _conv_in_kernel(x_ref, g_ref, m_ref, wb_ref, wc_ref, wv_ref, gb_ref, p_ref):
    m = m_ref[0]
    h = _modulate(x_ref[...], g_ref[...], m[0:1], m[1:2]).astype(BF16)
    gb_ref[...] = _dot(h, wb_ref[...]).astype(gb_ref.dtype)
    p_ref[...] = (_dot(h, wc_ref[...]) * _dot(h, wv_ref[...])).astype(p_ref.dtype)


def _conv_out_kernel(gb_ref, p_ref, pp_ref, pn_ref, k_ref, cb_ref, w_ref, x_ref, m_ref, o_ref):
    i = pl.program_id(0)
    last = pl.num_programs(0) - 1
    p = p_ref[...].astype(F32)
    rows = lax.broadcasted_iota(jnp.int32, p.shape, 0)
    prev_row = jnp.where(i >= 2, pp_ref[7:8, :].astype(F32), 0.0)
    next_row = jnp.where(jnp.logical_and(i >= 1, i < last), pn_ref[0:1, :].astype(F32), 0.0)
    p_dn = jnp.where(rows == 0, prev_row, pltpu.roll(p, 1, 0))
    p_up = jnp.where(rows == TM - 1, next_row, pltpu.roll(p, TM - 1, 0))
    z = k_ref[0:1, :] * p_dn + k_ref[1:2, :] * p + k_ref[2:3, :] * p_up + cb_ref[...]
    q = (gb_ref[...].astype(F32) * z).astype(BF16)
    o_ref[...] = x_ref[...] + m_ref[0][2:3] * _dot(q, w_ref[...])


def _conv_layer(xa, mods, g1, w_in, conv_k, conv_b, w_out):
    t, d = xa.shape
    tn = 1024
    nt = d // tn
    w_in = w_in.astype(BF16)
    gb, p = pl.pallas_call(
        _conv_in_kernel,
        grid=(nt, t // TM),
        in_specs=[pl.BlockSpec((TM, d), lambda j, i: (i, 0)),
                  pl.BlockSpec((1, d), lambda j, i: (0, 0)),
                  pl.BlockSpec((1, 6, d), lambda j, i: (_mod_row(i), 0, 0)),
                  pl.BlockSpec((d, tn), lambda j, i: (0, j)),
                  pl.BlockSpec((d, tn), lambda j, i: (0, nt + j)),
                  pl.BlockSpec((d, tn), lambda j, i: (0, 2 * nt + j))],
        out_specs=[pl.BlockSpec((TM, tn), lambda j, i: (i, j))] * 2,
        out_shape=[jax.ShapeDtypeStruct((t, d), BF16)] * 2,
        compiler_params=_params(("parallel", "parallel")),
        name="conv_in",
    )(xa, g1.reshape(1, d), mods, w_in, w_in, w_in)
    r8 = TM // 8
    n8 = t // 8
    return pl.pallas_call(
        _conv_out_kernel,
        grid=(t // TM,),
        in_specs=[pl.BlockSpec((TM, d), lambda i: (i, 0)),
                  pl.BlockSpec((TM, d), lambda i: (i, 0)),
                  pl.BlockSpec((8, d), lambda i: (jnp.maximum(i * r8 - 1, 0), 0)),
                  pl.BlockSpec((8, d), lambda i: (jnp.minimum((i + 1) * r8, n8 - 1), 0)),
                  pl.BlockSpec((3, d), lambda i: (0, 0)),
                  pl.BlockSpec((1, d), lambda i: (0, 0)),
                  pl.BlockSpec((d, d), lambda i: (0, 0)),
                  pl.BlockSpec((TM, d), lambda i: (i, 0)),
                  pl.BlockSpec((1, 6, d), lambda i: (_mod_row(i), 0, 0))],
        out_specs=pl.BlockSpec((TM, d), lambda i: (i, 0)),
        out_shape=jax.ShapeDtypeStruct((t, d), F32),
        compiler_params=_params(("parallel",)),
        name="conv_out",
    )(gb, p, p, p, conv_k, conv_b.reshape(1, d), w_out.astype(BF16), xa, mods)


def _rope_tables(n_lat):
    rows = n_lat // GRID_W
    r = jnp.repeat(jnp.arange(rows, dtype=F32), GRID_W)
    col = jnp.tile(jnp.arange(GRID_W, dtype=F32), rows)
    inv = ROPE_THETA ** (-jnp.arange(ROPE_AXIS_PAIRS, dtype=F32) / ROPE_AXIS_PAIRS)
    ang = jnp.concatenate([r[:, None] * inv, col[:, None] * inv], axis=-1)
    cos, sin = jnp.cos(ang), jnp.sin(ang)
    cos_f = jnp.concatenate([cos, cos], axis=-1)
    sin_f = jnp.concatenate([-sin, sin], axis=-1)
    cos_f = jnp.concatenate([jnp.ones((CTX_LEN, HEAD_DIM), F32), cos_f], axis=0)
    sin_f = jnp.concatenate([jnp.zeros((CTX_LEN, HEAD_DIM), F32), sin_f], axis=0)
    return cos_f, sin_f


def _qkv_kernel(x_ref, g_ref, m_ref, w_ref, qg_ref, kg_ref, cos_ref, sin_ref, q_ref, k_ref, v_ref):
    m = m_ref[0]
    h = _modulate(x_ref[...], g_ref[...], m[0:1], m[1:2]).astype(BF16)
    a = _dot(h, w_ref[...])
    cos, sin = cos_ref[...], sin_ref[...]
    nq = N_HEADS * HEAD_DIM
    nk = N_KV_HEADS * HEAD_DIM

    def norm_rope(u, gain):
        ms = jnp.mean(u * u, axis=-1, keepdims=True)
        un = u * lax.rsqrt(ms + NORM_EPS) * gain
        return un * cos + pltpu.roll(un, HEAD_DIM // 2, 1) * sin

    for hh in range(N_HEADS):
        sl = slice(hh * HEAD_DIM, (hh + 1) * HEAD_DIM)
        q_ref[:, sl] = (norm_rope(a[:, sl], qg_ref[...]) * (HEAD_DIM ** -0.5 * math.log2(math.e))).astype(q_ref.dtype)
    for hh in range(N_KV_HEADS):
        sl = slice(hh * HEAD_DIM, (hh + 1) * HEAD_DIM)
        k_ref[:, sl] = norm_rope(a[:, nq + hh * HEAD_DIM:nq + (hh + 1) * HEAD_DIM], kg_ref[...]).astype(k_ref.dtype)
    v_ref[...] = a[:, nq + nk:].astype(v_ref.dtype)


def _attn_kernel(q_ref, k_ref, v_ref, o_ref, s_ref, m_ref, l_ref, acc_ref):
    i = pl.program_id(1)
    t = k_ref.shape[0]
    nt_dims = (((1,), (1,)), ((), ()))
    m_ref[...] = jnp.full(m_ref.shape, -jnp.inf, F32)
    l_ref[...] = jnp.zeros(l_ref.shape, F32)
    acc_ref[...] = jnp.zeros(acc_ref.shape, F32)

    def scores(slot, start, size):
        kc = k_ref[pl.ds(start, size), :]
        for j in range(Q_PER_KV):
            s_ref[slot, j, :, 0:size] = lax.dot_general(q_ref[:, j * HEAD_DIM:(j + 1) * HEAD_DIM], kc, nt_dims,
                                                        preferred_element_type=F32)

    def update(slot, start, size):
        vc = v_ref[pl.ds(start, size), :]
        for j in range(Q_PER_KV):
            s = s_ref[slot, j, :, 0:size]
            m_old = m_ref[j]
            m_new = jnp.maximum(m_old, jnp.max(s, axis=-1, keepdims=True))
            alpha = jnp.exp2(m_old - m_new)
            p = jnp.exp2(s - jnp.tile(m_new, (1, size // LANES)))
            l_ref[j] = alpha * l_ref[j] + jnp.sum(p, axis=-1, keepdims=True)
            acc_ref[j] = alpha * acc_ref[j] + _dot(p.astype(BF16), vc)
            m_ref[j] = m_new

    scores(0, 0, CTX_LEN)
    update(0, 0, CTX_LEN)

    @pl.when(i > 0)
    def _():
        n_pairs = (t - CTX_LEN) // (2 * ATT_TK)
        scores(0, CTX_LEN, ATT_TK)

        def body(c, carry):
            base = pl.multiple_of(CTX_LEN + 2 * c * ATT_TK, CTX_LEN)
            scores(1, base + ATT_TK, ATT_TK)
            update(0, base, ATT_TK)
            nxt = pl.multiple_of(jnp.minimum(base + 2 * ATT_TK, t - ATT_TK), CTX_LEN)
            scores(0, nxt, ATT_TK)
            update(1, base + ATT_TK, ATT_TK)
            return carry

        lax.fori_loop(0, n_pairs, body, 0)

    for j in range(Q_PER_KV):
        o_ref[:, j * HEAD_DIM:(j + 1) * HEAD_DIM] = (acc_ref[j] / l_ref[j]).astype(o_ref.dtype)


def _attn_layer(xa, mods, g1, wq, wk, wv, wo, q_g, k_g):
    t, d = xa.shape
    nq, nk = N_HEADS * HEAD_DIM, N_KV_HEADS * HEAD_DIM
    w = jnp.concatenate([wq, wk, wv], axis=1).astype(BF16)
    cos_f, sin_f = _rope_tables(t - CTX_LEN)
    q, k, v = pl.pallas_call(
        _qkv_kernel,
        grid=(t // TM,),
        in_specs=[pl.BlockSpec((TM, d), lambda i: (i, 0)),
                  pl.BlockSpec((1, d), lambda i: (0, 0)),
                  pl.BlockSpec((1, 6, d), lambda i: (_mod_row(i), 0, 0)),
                  pl.BlockSpec((d, nq + 2 * nk), lambda i: (0, 0)),
                  pl.BlockSpec((1, HEAD_DIM), lambda i: (0, 0)),
                  pl.BlockSpec((1, HEAD_DIM), lambda i: (0, 0)),
                  pl.BlockSpec((TM, HEAD_DIM), lambda i: (i, 0)),
                  pl.BlockSpec((TM, HEAD_DIM), lambda i: (i, 0))],
        out_specs=[pl.BlockSpec((TM, nq), lambda i: (i, 0)),
                   pl.BlockSpec((TM, nk), lambda i: (i, 0)),
                   pl.BlockSpec((TM, nk), lambda i: (i, 0))],
        out_shape=[jax.ShapeDtypeStruct((t, nq), BF16),
                   jax.ShapeDtypeStruct((t, nk), BF16),
                   jax.ShapeDtypeStruct((t, nk), BF16)],
        compiler_params=_params(("parallel",)),
        name="qkv",
    )(xa, g1.reshape(1, d), mods, w, q_g.reshape(1, HEAD_DIM), k_g.reshape(1, HEAD_DIM), cos_f, sin_f)
    qw = Q_PER_KV * HEAD_DIM
    o = pl.pallas_call(
        _attn_kernel,
        grid=(N_KV_HEADS, t // TM),
        in_specs=[pl.BlockSpec((TM, qw), lambda n, i: (i, n)),
                  pl.BlockSpec((t, HEAD_DIM), lambda n, i: (0, n)),
                  pl.BlockSpec((t, HEAD_DIM), lambda n, i: (0, n))],
        out_specs=pl.BlockSpec((TM, qw), lambda n, i: (i, n)),
        out_shape=jax.ShapeDtypeStruct((t, nq), BF16),
        scratch_shapes=[pltpu.VMEM((2, Q_PER_KV, TM, ATT_TK), F32),
                        pltpu.VMEM((Q_PER_KV, TM, LANES), F32),
                        pltpu.VMEM((Q_PER_KV, TM, LANES), F32),
                        pltpu.VMEM((Q_PER_KV, TM, HEAD_DIM), F32)],
        compiler_params=_params(("parallel", "parallel")),
        name="attention",
    )(q, k, v)
    return _mm_resid(o, wo.astype(BF16), jnp.zeros((d,), F32), xa, mods, 2)


def _s5_matrices(a_re, a_im, log_dt, b_re, b_im, c_re, c_im, reverse):
    l, g, p, c = SSM_SUB, SSM_GROUPS, SSM_STATE, SSM_GROUP_DIM
    a = lax.complex(a_re.astype(F32), a_im.astype(F32))
    adt = a * jnp.exp(log_dt.astype(F32))[:, None]
    a_bar = jnp.exp(adt)
    b_bar = ((a_bar - 1) / a)[..., None] * lax.complex(b_re.astype(F32), b_im.astype(F32))
    c_mat = lax.complex(c_re.astype(F32), c_im.astype(F32))
    pw = jnp.exp(adt[:, None, :] * jnp.arange(l + 1, dtype=F32)[None, :, None])
    kern = jnp.einsum('gcp,gkp,gpd->gkcd', c_mat, pw[:, :l], b_bar,
                      precision=lax.Precision.HIGHEST).real
    e = pw[:, l - 1 - jnp.arange(l), :][:, :, None, :] * jnp.transpose(b_bar, (0, 2, 1))[:, None, :, :]
    cp = jnp.transpose(c_mat[:, None, :, :] * pw[:, 1:, None, :], (0, 3, 1, 2))

    nb, gpb = SSM_CB, SSM_GPB
    ch_grp = jnp.arange(LANES) // c
    st_grp = jnp.arange(gpb * p) // p
    kfull = jnp.transpose(kern, (0, 1, 3, 2)).reshape(nb, gpb, l, c, c)
    kfull = jnp.transpose(kfull, (0, 2, 1, 3, 4)).reshape(nb, l, LANES, c)
    kfull = jnp.tile(kfull, (1, 1, 1, gpb)) * (ch_grp[:, None] == ch_grp[None, :])
    lag = jnp.arange(l)[None, :] - jnp.arange(l)[:, None]
    m = jnp.where((lag >= 0)[None, :, :, None, None], kfull[:, jnp.clip(lag, 0, l - 1)], 0.0)

    def widen_e(x):
        x = jnp.transpose(x.reshape(nb, gpb, l, c, p), (0, 2, 1, 3, 4)).reshape(nb, l, LANES, p)
        return jnp.tile(x, (1, 1, 1, gpb)) * (ch_grp[:, None] == st_grp[None, :])

    def widen_c(x):
        x = x.reshape(nb, gpb * p, l, c)
        return jnp.tile(x, (1, 1, 1, gpb)) * (st_grp[:, None, None] == ch_grp[None, None, :])

    e_re, e_im = widen_e(e.real), widen_e(e.imag)
    cp_re, cp_im = widen_c(cp.real), widen_c(-cp.imag)
    if reverse:
        m = jnp.flip(m, (1, 2))
        e_re, e_im = jnp.flip(e_re, 1), jnp.flip(e_im, 1)
        cp_re, cp_im = jnp.flip(cp_re, 2), jnp.flip(cp_im, 2)
    mbig = jnp.transpose(m.astype(BF16), (0, 1, 3, 2, 4)).reshape(nb, l * LANES, l * LANES)
    w1 = jnp.concatenate([e_re, e_im], axis=3).astype(BF16).reshape(nb, l * LANES, 2 * gpb * p)
    w2 = jnp.concatenate([mbig, cp_re.astype(BF16).reshape(nb, gpb * p, l * LANES),
                          cp_im.astype(BF16).reshape(nb, gpb * p, l * LANES)], axis=1)
    al = pw[:, l].reshape(SSM_CB, 1, SSM_GPB * p)
    return w1, w2, al.real, al.imag


def _s5_scan_kernel(*refs, reverse):
    l = SSM_SUB
    u_refs = refs[:l]
    w1_ref, w2_ref, ar_ref, ai_ref, h0_ref, y_ref, hfin_ref, v_ref, h_ref = refs[l:]
    ns = ar_ref.shape[-1]
    rows = v_ref.shape[0]

    @pl.when(pl.program_id(1) == 0)
    def _():
        h_ref[...] = h0_ref[0]

    u = jnp.concatenate([r[...] for r in u_refs], axis=1)
    v_ref[...] = _dot(u, w1_ref[0])
    ar, ai = ar_ref[0], ai_ref[0]

    def body(step, carry):
        hr, hi = carry
        j = rows - 1 - step if reverse else step
        vr = v_ref[pl.ds(j, 1), 0:ns]
        vi = v_ref[pl.ds(j, 1), ns:2 * ns]
        v_ref[pl.ds(j, 1), 0:ns] = hr
        v_ref[pl.ds(j, 1), ns:2 * ns] = hi
        return ar * hr - ai * hi + vr, ar * hi + ai * hr + vi

    hr, hi = lax.fori_loop(0, rows, body, (h_ref[0:1, :], h_ref[1:2, :]))
    h_ref[0:1, :] = hr
    h_ref[1:2, :] = hi
    hfin_ref[0] = h_ref[...]
    y = _dot(jnp.concatenate([u, v_ref[...].astype(BF16)], axis=1), w2_ref[0])
    for t in range(l):
        y_ref[t] = y[:, t * LANES:(t + 1) * LANES]


def _s5_scan(seq, mats, h0, reverse):
    t, d = seq.shape
    l = SSM_SUB
    w1, w2, al_re, al_im = mats
    n_sub = t // l
    rows = max(r for r in range(8, min(n_sub, 512) + 1, 8) if n_sub % r == 0)
    nr = n_sub // rows
    ns = SSM_GPB * SSM_STATE
    u2 = seq.reshape(n_sub, l * d)
    row_blk = (lambda r: nr - 1 - r) if reverse else (lambda r: r)
    blk = lambda s: pl.BlockSpec((rows, LANES), lambda cb, r, s=s: (row_blk(r), s * SSM_CB + cb))
    return pl.pallas_call(
        functools.partial(_s5_scan_kernel, reverse=reverse),
        grid=(SSM_CB, nr),
        in_specs=[blk(s) for s in range(l)] + [
            pl.BlockSpec((1,) + w1.shape[1:], lambda cb, r: (cb, 0, 0)),
            pl.BlockSpec((1,) + w2.shape[1:], lambda cb, r: (cb, 0, 0)),
            pl.BlockSpec((1, 1, ns), lambda cb, r: (cb, 0, 0)),
            pl.BlockSpec((1, 1, ns), lambda cb, r: (cb, 0, 0)),
            pl.BlockSpec((1, 2, ns), lambda cb, r: (cb, 0, 0))],
        out_specs=[pl.BlockSpec((l, rows, LANES), lambda cb, r: (0, row_blk(r), cb)),
                   pl.BlockSpec((1, 2, ns), lambda cb, r: (cb, 0, 0))],
        out_shape=[jax.ShapeDtypeStruct((l, n_sub, d), F32),
                   jax.ShapeDtypeStruct((SSM_CB, 2, ns), F32)],
        scratch_shapes=[pltpu.VMEM((rows, 2 * ns), F32), pltpu.VMEM((2, ns), F32)],
        compiler_params=_params(("parallel", "arbitrary")),
        name="s5_scan",
    )(*([u2] * l), w1, w2, al_re, al_im, h0)


def _gelu_tanh(x):
    return 0.5 * x * (1.0 + jnp.tanh(math.sqrt(2.0 / math.pi) * (x + 0.044715 * x * x * x)))


def _s5_glu_kernel(x_ref, g_ref, m_ref, d_ref, yf_ref, yb_ref, wa_ref, wg_ref, ba_ref, bg_ref, xo_ref, mo_ref, o_ref):
    m = m_ref[0]
    h = _modulate(x_ref[...], g_ref[...], m[0:1], m[1:2])
    z = _gelu_tanh(d_ref[...] * h + yf_ref[0] + yb_ref[0]).astype(BF16)
    a = _dot(z, wa_ref[...]) + ba_ref[...]
    g = _dot(z, wg_ref[...]) + bg_ref[...]
    o_ref[...] = xo_ref[...] + mo_ref[0][2:3] * (a * jax.nn.sigmoid(g))


def _s5_layer(xa, mods, g1, a_re, a_im, log_dt, b_re, b_im, c_re, c_im, d_skip, glu_w, glu_b):
    t, d = xa.shape
    l = SSM_SUB
    h = _prep(xa, g1, mods, 0)
    h_ctx, h_lat = h[:CTX_LEN], h[CTX_LEN:]
    x_lat = xa[CTX_LEN:]
    ns = SSM_GPB * SSM_STATE
    ys = []
    for direction in range(2):
        rev = direction == 1
        mats = _s5_matrices(a_re[direction], a_im[direction], log_dt[direction], b_re[direction], b_im[direction],
                            c_re[direction], c_im[direction], rev)
        _, state = _s5_scan(h_ctx, mats, jnp.zeros((SSM_CB, 2, ns), F32), rev)
        y, _ = _s5_scan(h_lat, mats, state, rev)
        ys.append(y)
    tn = 1024
    nt = d // tn
    n_sub = (t - CTX_LEN) // l
    assert n_sub % TM == 0
    glu_w = glu_w.astype(BF16)
    out = pl.pallas_call(
        _s5_glu_kernel,
        grid=(nt, n_sub // TM, l),
        in_specs=[pl.BlockSpec((TM, d), lambda j, i, s: (i, s)),
                  pl.BlockSpec((1, d), lambda j, i, s: (0, 0)),
                  pl.BlockSpec((1, 6, d), lambda j, i, s: (1, 0, 0)),
                  pl.BlockSpec((1, d), lambda j, i, s: (0, 0)),
                  pl.BlockSpec((1, TM, d), lambda j, i, s: (s, i, 0)),
                  pl.BlockSpec((1, TM, d), lambda j, i, s: (s, i, 0)),
                  pl.BlockSpec((d, tn), lambda j, i, s: (0, j)),
                  pl.BlockSpec((d, tn), lambda j, i, s: (0, nt + j)),
                  pl.BlockSpec((1, tn), lambda j, i, s: (0, j)),
                  pl.BlockSpec((1, tn), lambda j, i, s: (0, nt + j)),
                  pl.BlockSpec((TM, tn), lambda j, i, s: (i, s * nt + j)),
                  pl.BlockSpec((1, 6, tn), lambda j, i, s: (1, 0, j))],
        out_specs=pl.BlockSpec((TM, tn), lambda j, i, s: (i, s * nt + j)),
        out_shape=jax.ShapeDtypeStruct((n_sub, l * d), F32),
        compiler_params=_params(("parallel", "parallel", "parallel")),
        name="s5_glu",
    )(x_lat.reshape(n_sub, l * d), g1.reshape(1, d), mods, d_skip.reshape(1, d), ys[0], ys[1], glu_w, glu_w,
      glu_b.reshape(1, 2 * d), glu_b.reshape(1, 2 * d), x_lat.reshape(n_sub, l * d), mods)
    return out.reshape(t - CTX_LEN, d)


def _router_kernel(x_ref, g_ref, m_ref, wr_ref, br_ref, tri_ref, h_ref, idx_ref, gate_ref, rank_ref, cnt_ref, base_ref):
    i = pl.program_id(0)

    @pl.when(i == 0)
    def _():
        base_ref[...] = jnp.zeros(base_ref.shape, F32)

    m = m_ref[0]
    h = _modulate(x_ref[...], g_ref[...], m[3:4], m[4:5])
    h_ref[...] = h.astype(h_ref.dtype)
    logits = lax.dot_general(wr_ref[...], h, (((1,), (1,)), ((), ())), precision=lax.Precision.HIGHEST,
                             preferred_element_type=F32) + br_ref[...]
    e_iota = lax.broadcasted_iota(jnp.int32, logits.shape, 0).astype(F32)
    base = base_ref[...]
    tops, idxs, ranks = [], [], []
    for _ in range(TOP_K):
        top = jnp.max(logits, axis=0, keepdims=True)
        idx = jnp.min(jnp.where(logits == top, e_iota, float(N_EXPERTS)), axis=0, keepdims=True)
        sel = e_iota == idx
        logits = jnp.where(sel, -jnp.inf, logits)
        onehot = sel.astype(BF16)
        before = _dot(onehot, tri_ref[...])
        ranks.append(jnp.sum(jnp.where(sel, base + before, 0.0), axis=0, keepdims=True))
        base = base + jnp.sum(sel.astype(F32), axis=1, keepdims=True)
        tops.append(top)
        idxs.append(idx)
    base_ref[...] = base
    cnt_ref[...] = jnp.broadcast_to(base, cnt_ref.shape)
    ex = [jnp.exp(v - tops[0]) for v in tops]
    den = ex[0] + ex[1] + ex[2] + ex[3]
    idx_ref[...] = jnp.concatenate(idxs, axis=0).astype(jnp.int32)
    gate_ref[...] = jnp.concatenate([v / den for v in ex], axis=0)
    rank_ref[...] = jnp.concatenate(ranks, axis=0).astype(jnp.int32)


def _ffn_kernel(be_ref, x_ref, w1_ref, b1_ref, w2_ref, b2_ref, o_ref, w1b_ref, w2b_ref):
    i = pl.program_id(0)
    new_expert = jnp.logical_or(i == 0, be_ref[0, i] != be_ref[0, jnp.maximum(i - 1, 0)])

    @pl.when(new_expert)
    def _():
        def cast(ref, dst):
            n_chunks = ref.shape[2] // WCAST_ROWS

            def body(r, carry):
                rows = pl.ds(pl.multiple_of(r * WCAST_ROWS, WCAST_ROWS), WCAST_ROWS)
                dst[rows, :] = ref[0, 0, rows, :].astype(BF16)
                return carry

            lax.fori_loop(0, n_chunks, body, 0)

        cast(w1_ref, w1b_ref)
        cast(w2_ref, w2b_ref)

    @pl.when(be_ref[1, i] > 0)
    def _():
        h = _dot(x_ref[...], w1b_ref[...]) + b1_ref[0]
        gate = jnp.minimum(h[:, :D_EXPERT], SWIGLU_LIMIT)
        lin = jnp.clip(h[:, D_EXPERT:], -SWIGLU_LIMIT, SWIGLU_LIMIT)
        act = ((lin + 1.0) * gate * jax.nn.sigmoid(SWIGLU_ALPHA * gate)).astype(BF16)
        o_ref[...] = (_dot(act, w2b_ref[...]) + b2_ref[0]).astype(o_ref.dtype)


def _combine_kernel(y_ref, gt_ref, x_ref, m_ref, o_ref):
    gt = gt_ref[...]
    acc = gt[:, 0:1] * y_ref[0].astype(F32)
    for k in range(1, TOP_K):
        acc = acc + gt[:, k:k + 1] * y_ref[k].astype(F32)
    o_ref[...] = x_ref[...] + m_ref[0][5:6] * acc


def _moe_layer(xa, mods, g2, w_r, b_r, w1, b1, w2, b2, layer):
    t, d = xa.shape
    nb = t // TM
    tri = jnp.asarray(np.triu(np.ones((TM, TM), np.float32), 1), BF16)
    h2, idx, gates, rank, cnt = pl.pallas_call(
        _router_kernel,
        grid=(nb,),
        in_specs=[pl.BlockSpec((TM, d), lambda i: (i, 0)),
                  pl.BlockSpec((1, d), lambda i: (0, 0)),
                  pl.BlockSpec((1, 6, d), lambda i: (_mod_row(i), 0, 0)),
                  pl.BlockSpec((N_EXPERTS, d), lambda i: (0, 0)),
                  pl.BlockSpec((N_EXPERTS, 1), lambda i: (0, 0)),
                  pl.BlockSpec((TM, TM), lambda i: (0, 0))],
        out_specs=[pl.BlockSpec((TM, d), lambda i: (i, 0)),
                   pl.BlockSpec((TOP_K, TM), lambda i: (0, i)),
                   pl.BlockSpec((TOP_K, TM), lambda i: (0, i)),
                   pl.BlockSpec((TOP_K, TM), lambda i: (0, i)),
                   pl.BlockSpec((N_EXPERTS, LANES), lambda i: (0, 0))],
        out_shape=[jax.ShapeDtypeStruct((t, d), BF16),
                   jax.ShapeDtypeStruct((TOP_K, t), jnp.int32),
                   jax.ShapeDtypeStruct((TOP_K, t), F32),
                   jax.ShapeDtypeStruct((TOP_K, t), jnp.int32),
                   jax.ShapeDtypeStruct((N_EXPERTS, LANES), F32)],
        scratch_shapes=[pltpu.VMEM((N_EXPERTS, 1), F32)],
        compiler_params=_params(("arbitrary",)),
        name="router",
    )(xa, g2.reshape(1, d), mods, w_r.T, b_r.reshape(N_EXPERTS, 1), tri)

    counts = cnt[:, 0].astype(jnp.int32)
    padded = (counts + MOE_TM - 1) // MOE_TM * MOE_TM
    pend = jnp.cumsum(padded)
    pstart = pend - padded
    dest = pstart[idx] + rank
    n_rows = -(-t * TOP_K // MOE_TM) * MOE_TM + N_EXPERTS * MOE_TM
    n_blocks = n_rows // MOE_TM
    blk_start = jnp.arange(n_blocks, dtype=jnp.int32) * MOE_TM
    blk_used = (blk_start < pend[-1]).astype(jnp.int32)
    blk_e = jnp.minimum(jnp.sum((pend[None, :] <= blk_start[:, None]).astype(jnp.int32), axis=1), N_EXPERTS - 1)
    last_e = blk_e[jnp.maximum(pend[-1] // MOE_TM - 1, 0)]
    blk_e = jnp.where(blk_used > 0, blk_e, last_e)
    be = jnp.stack([blk_e, blk_used])
    tok = jnp.broadcast_to(jnp.arange(t, dtype=jnp.int32)[None, :], (TOP_K, t))
    row_tok = (jnp.arange(n_rows, dtype=jnp.int32) % t).at[dest.reshape(-1)].set(
        tok.reshape(-1), unique_indices=True, mode='promise_in_bounds')
    xs = h2.at[row_tok].get(mode='promise_in_bounds')

    ys = pl.pallas_call(
        _ffn_kernel,
        grid_spec=pltpu.PrefetchScalarGridSpec(
            num_scalar_prefetch=1,
            grid=(n_blocks,),
            in_specs=[pl.BlockSpec((MOE_TM, d), lambda i, be: (i, 0)),
                      pl.BlockSpec((1, 1, d, 2 * D_EXPERT), lambda i, be: (layer, be[0, i], 0, 0),
                                   pipeline_mode=pl.Buffered(1)),
                      pl.BlockSpec((1, 1, 2 * D_EXPERT), lambda i, be: (be[0, i], 0, 0)),
                      pl.BlockSpec((1, 1, D_EXPERT, d), lambda i, be: (layer, be[0, i], 0, 0),
                                   pipeline_mode=pl.Buffered(1)),
                      pl.BlockSpec((1, 1, d), lambda i, be: (be[0, i], 0, 0))],
            out_specs=pl.BlockSpec((MOE_TM, d), lambda i, be: (i, 0)),
            scratch_shapes=[pltpu.VMEM((d, 2 * D_EXPERT), BF16), pltpu.VMEM((D_EXPERT, d), BF16)]),
        out_shape=jax.ShapeDtypeStruct((n_rows, d), BF16),
        compiler_params=_params(("arbitrary",)),
        name="moe_ffn",
    )(be, xs, w1, b1.reshape(N_EXPERTS, 1, 2 * D_EXPERT), w2, b2.reshape(N_EXPERTS, 1, d))

    y4 = ys.at[dest.reshape(-1)].get(mode='promise_in_bounds', unique_indices=True).reshape(TOP_K, t, d)
    return pl.pallas_call(
        _combine_kernel,
        grid=(nb,),
        in_specs=[pl.BlockSpec((TOP_K, TM, d), lambda i: (0, i, 0)),
                  pl.BlockSpec((TM, TOP_K), lambda i: (i, 0)),
                  pl.BlockSpec((TM, d), lambda i: (i, 0)),
                  pl.BlockSpec((1, 6, d), lambda i: (_mod_row(i), 0, 0))],
        out_specs=pl.BlockSpec((TM, d), lambda i: (i, 0)),
        out_shape=jax.ShapeDtypeStruct((t, d), F32),
        compiler_params=_params(("parallel",)),
        name="moe_combine",
    )(y4, gates.T, xa, mods)


def kernel(x, c, ctx, c_ctx, ada_w, ada_b, norm1_g, norm2_g, fourier_w_out, fourier_b_out, conv_w_in, conv_k, conv_b, conv_w_out, attn_wq, attn_wk, attn_wv, attn_wo, attn_q_g, attn_k_g, ssm_a_re, ssm_a_im, ssm_log_dt, ssm_b_re, ssm_b_im, ssm_c_re, ssm_c_im, ssm_d, ssm_glu_w, ssm_glu_b, router_w, router_b, moe_w1, moe_b1, moe_w2, moe_b2):
    bsz, n, d = x.shape
    assert bsz == 1 and d == D_MODEL and ctx.shape[1] == CTX_LEN and n % TM == 0
    depth = ada_w.shape[0]
    cond8 = jnp.zeros((8, d), F32).at[0].set(c_ctx).at[1].set(c[0])
    mods = _ada_all(cond8, ada_w, ada_b).reshape(depth, 8, 6, d)
    assert depth == 4
    xa = jnp.concatenate([ctx[0], x[0]], axis=0)
    for i in range(depth):
        m, j = i % 4, i // 4
        mods_i = mods[i]
        if m == 0:
            xa = _fourier_layer(xa, mods[i], norm1_g[i], fourier_w_out[j], fourier_b_out[j])
        elif m == 1:
            xa = _conv_layer(xa, mods[i], norm1_g[i], conv_w_in[j], conv_k[j], conv_b[j], conv_w_out[j])
        elif m == 2:
            xa = _attn_layer(xa, mods[i], norm1_g[i], attn_wq[j], attn_wk[j], attn_wv[j], attn_wo[j],
                             attn_q_g[j], attn_k_g[j])
        else:
            xa = _s5_layer(xa, mods[i], norm1_g[i], ssm_a_re[j], ssm_a_im[j], ssm_log_dt[j], ssm_b_re[j],
                           ssm_b_im[j], ssm_c_re[j], ssm_c_im[j], ssm_d[j], ssm_glu_w[j], ssm_glu_b[j])
            mods_i = mods_i.at[0].set(mods_i[1])
        xa = _moe_layer(xa, mods_i, norm2_g[i], router_w[i], router_b[i], moe_w1, moe_b1[i], moe_w2, moe_b2[i], i)
    return xa[None]
```

```python
import functools
import math

import numpy as np
import jax
import jax.numpy as jnp
from jax import lax
from jax.experimental import pallas as pl
from jax.experimental.pallas import tpu as pltpu

F32 = jnp.float32
BF16 = jnp.bfloat16

D_MODEL = 2048
GRID_W = 64
CTX_LEN = 256
FOURIER_GROUPS = 8
FOURIER_GROUP_DIM = D_MODEL // FOURIER_GROUPS
HEAD_DIM = 128
N_HEADS = D_MODEL // HEAD_DIM
N_KV_HEADS = 4
Q_PER_KV = N_HEADS // N_KV_HEADS
ROPE_AXIS_PAIRS = HEAD_DIM // 4
ROPE_THETA = 10000.0
SSM_GROUP_DIM = 16
SSM_GROUPS = D_MODEL // SSM_GROUP_DIM
SSM_STATE = 64
N_EXPERTS = 32
TOP_K = 4
D_EXPERT = 1024
SWIGLU_LIMIT = 7.0
SWIGLU_ALPHA = 1.702
NORM_EPS = 1e-6

LANES = 128
TM = 256
DFT_N2 = 128
SSM_SUB = 8
SSM_CB = D_MODEL // LANES
SSM_GPB = LANES // SSM_GROUP_DIM
MOE_TM = 512
ATT_TK = 1024
WCAST_ROWS = 256
VMEM_LIMIT = 56 * 2 ** 20


def _params(sem, vmem=VMEM_LIMIT):
    return pltpu.CompilerParams(dimension_semantics=sem, vmem_limit_bytes=vmem)


def _mod_row(i):
    return jnp.minimum(i, 1)


def _modulate(x, g, shift, scale):
    ms = jnp.mean(x * x, axis=-1, keepdims=True)
    return (x * lax.rsqrt(ms + NORM_EPS) * g) * (1.0 + scale) + shift


def _dot(a, b):
    return jnp.dot(a, b, preferred_element_type=F32)


def _ada_kernel(c_ref, w_ref, b_ref, o_ref):
    c = c_ref[...]
    s = c * jax.nn.sigmoid(c)
    o_ref[0] = jnp.dot(s, w_ref[0], precision=lax.Precision.HIGHEST, preferred_element_type=F32) + b_ref[0]


def _ada_all(cond8, ada_w, ada_b):
    depth, d, n6 = ada_w.shape
    tn = 1024
    return pl.pallas_call(
        _ada_kernel,
        grid=(depth, n6 // tn),
        in_specs=[pl.BlockSpec((8, d), lambda l, j: (0, 0)),
                  pl.BlockSpec((1, d, tn), lambda l, j: (l, 0, j)),
                  pl.BlockSpec((1, 1, tn), lambda l, j: (l, 0, j))],
        out_specs=pl.BlockSpec((1, 8, tn), lambda l, j: (l, 0, j)),
        out_shape=jax.ShapeDtypeStruct((depth, 8, n6), F32),
        compiler_params=_params(("parallel", "parallel")),
        name="ada",
    )(cond8, ada_w, ada_b.reshape(depth, 1, n6))


def _prep_kernel(x_ref, g_ref, m_ref, o_ref, *, which):
    m = m_ref[0]
    h = _modulate(x_ref[...], g_ref[...], m[3 * which:3 * which + 1], m[3 * which + 1:3 * which + 2])
    o_ref[...] = h.astype(o_ref.dtype)


def _prep(xa, g, mods, which):
    t, d = xa.shape
    return pl.pallas_call(
        functools.partial(_prep_kernel, which=which),
        grid=(t // TM,),
        in_specs=[pl.BlockSpec((TM, d), lambda i: (i, 0)),
                  pl.BlockSpec((1, d), lambda i: (0, 0)),
                  pl.BlockSpec((1, 6, d), lambda i: (_mod_row(i), 0, 0))],
        out_specs=pl.BlockSpec((TM, d), lambda i: (i, 0)),
        out_shape=jax.ShapeDtypeStruct((t, d), BF16),
        compiler_params=_params(("parallel",)),
        name="prep",
    )(xa, g.reshape(1, d), mods)


def _mm_resid_kernel(a_ref, w_ref, b_ref, x_ref, m_ref, o_ref, *, gate_row):
    y = _dot(a_ref[...], w_ref[...]) + b_ref[...]
    o_ref[...] = x_ref[...] + m_ref[0][gate_row:gate_row + 1] * y


def _mm_resid(a, w, b, xa, mods, gate_row):
    t, d = xa.shape
    k = a.shape[1]
    return pl.pallas_call(
        functools.partial(_mm_resid_kernel, gate_row=gate_row),
        grid=(t // TM,),
        in_specs=[pl.BlockSpec((TM, k), lambda i: (i, 0)),
                  pl.BlockSpec((k, d), lambda i: (0, 0)),
                  pl.BlockSpec((1, d), lambda i: (0, 0)),
                  pl.BlockSpec((TM, d), lambda i: (i, 0)),
                  pl.BlockSpec((1, 6, d), lambda i: (_mod_row(i), 0, 0))],
        out_specs=pl.BlockSpec((TM, d), lambda i: (i, 0)),
        out_shape=jax.ShapeDtypeStruct((t, d), F32),
        compiler_params=_params(("parallel",)),
        name="mm_resid",
    )(a, w, b.reshape(1, d), xa, mods)


def _dft_tables(n1, n2):
    n = n1 * n2
    k1 = np.arange(n1, dtype=np.float64)
    t = (np.arange(n1, dtype=np.float64)[None, :] * n2 + np.arange(n2, dtype=np.float64)[:, None])
    ang = 2.0 * np.pi * k1[None, :, None] * t[:, None, :] / n
    g = np.concatenate([np.cos(ang), -np.sin(ang)], axis=1) / math.sqrt(n1)
    a2 = 2.0 * np.pi * np.outer(np.arange(n2), np.arange(n2)) / n2
    c2, s2 = np.cos(a2) / math.sqrt(n2), np.sin(a2) / math.sqrt(n2)
    f2 = np.block([[c2, s2], [-s2, c2]])
    return jnp.asarray(g, BF16), jnp.asarray(f2, BF16)


def _dft_stage1_kernel(x_ref, g_ref, re_ref, im_ref):
    n1 = x_ref.shape[0]
    a = _dot(g_ref[0], x_ref[...])
    re_ref[0] = a[:n1].astype(re_ref.dtype)
    im_ref[0] = a[n1:].astype(im_ref.dtype)


def _dft_stage2_kernel(re_ref, im_ref, f_ref, zre_ref, zim_ref):
    n2 = re_ref.shape[0]
    z = _dot(f_ref[...], jnp.concatenate([re_ref[...], im_ref[...]], axis=0))
    zre_ref[...] = z[:n2].astype(zre_ref.dtype)
    zim_ref[...] = z[n2:].astype(zim_ref.dtype)


def _pos_dft(h):
    n, d = h.shape
    if n <= 512:
        n1, n2 = n, 1
    else:
        n1, n2 = n // DFT_N2, DFT_N2
    g, f2 = _dft_tables(n1, n2)
    are, aim = pl.pallas_call(
        _dft_stage1_kernel,
        grid=(n2,),
        in_specs=[pl.BlockSpec((n1, d), lambda j: (0, j)),
                  pl.BlockSpec((1, 2 * n1, n1), lambda j: (j, 0, 0))],
        out_specs=[pl.BlockSpec((1, n1, d), lambda j: (j, 0, 0))] * 2,
        out_shape=[jax.ShapeDtypeStruct((n2, n1, d), BF16)] * 2,
        compiler_params=_params(("parallel",)),
        name="dft_stage1",
    )(h.reshape(n1, n2 * d), g)
    if n2 == 1:
        return are.reshape(n, d), aim.reshape(n, d)
    tn = 2048
    zre, zim = pl.pallas_call(
        _dft_stage2_kernel,
        grid=(n1 * d // tn,),
        in_specs=[pl.BlockSpec((n2, tn), lambda j: (0, j)),
                  pl.BlockSpec((n2, tn), lambda j: (0, j)),
                  pl.BlockSpec((2 * n2, 2 * n2), lambda j: (0, 0))],
        out_specs=[pl.BlockSpec((n2, tn), lambda j: (0, j))] * 2,
        out_shape=[jax.ShapeDtypeStruct((n2, n1 * d), BF16)] * 2,
        compiler_params=_params(("parallel",)),
        name="dft_stage2",
    )(are.reshape(n2, n1 * d), aim.reshape(n2, n1 * d), f2)
    return zre.reshape(n, d), zim.reshape(n, d)


def _fourier_out_kernel(zrc_ref, zic_ref, zrl_ref, zil_ref, cc_ref, sc_ref, w_ref, b_ref, x_ref, m_ref, o_ref):
    is_ctx = pl.program_id(0) == 0
    zre = jnp.where(is_ctx, zrc_ref[...], zrl_ref[...])
    zim = jnp.where(is_ctx, zic_ref[...], zil_ref[...])
    gd = FOURIER_GROUP_DIM
    parts = []
    for g in range(FOURIER_GROUPS):
        sl = slice(g * gd, (g + 1) * gd)
        parts.append(_dot(zre[:, sl], cc_ref[...]) + _dot(zim[:, sl], sc_ref[...]))
    f = jnp.concatenate(parts, axis=1).astype(BF16)
    y = _dot(f, w_ref[...]) + b_ref[...]
    o_ref[...] = x_ref[...] + m_ref[0][2:3] * y


def _fourier_layer(xa, mods, g1, w_out, b_out):
    t, d = xa.shape
    h = _prep(xa, g1, mods, 0)
    zrc, zic = _pos_dft(h[:CTX_LEN])
    zrl, zil = _pos_dft(h[CTX_LEN:])
    gd = FOURIER_GROUP_DIM
    ang = 2.0 * np.pi * np.outer(np.arange(gd), np.arange(gd)) / gd
    cc = jnp.asarray(np.cos(ang) / math.sqrt(gd), BF16)
    sc = jnp.asarray(np.sin(ang) / math.sqrt(gd), BF16)
    lat = lambda i: (jnp.maximum(i - 1, 0), 0)
    return pl.pallas_call(
        _fourier_out_kernel,
        grid=(t // TM,),
        in_specs=[pl.BlockSpec((TM, d), lambda i: (0, 0)),
                  pl.BlockSpec((TM, d), lambda i: (0, 0)),
                  pl.BlockSpec((TM, d), lat),
                  pl.BlockSpec((TM, d), lat),
                  pl.BlockSpec((gd, gd), lambda i: (0, 0)),
                  pl.BlockSpec((gd, gd), lambda i: (0, 0)),
                  pl.BlockSpec((d, d), lambda i: (0, 0)),
                  pl.BlockSpec((1, d), lambda i: (0, 0)),
                  pl.BlockSpec((TM, d), lambda i: (i, 0)),
                  pl.BlockSpec((1, 6, d), lambda i: (_mod_row(i), 0, 0))],
        out_specs=pl.BlockSpec((TM, d), lambda i: (i, 0)),
        out_shape=jax.ShapeDtypeStruct((t, d), F32),
        compiler_params=_params(("parallel",)),
        name="fourier_out",
    )(zrc, zic, zrl, zil, cc, sc, w_out.astype(BF16), b_out.reshape(1, d), xa, mods)


def _conv_in_kernel(x_ref, g_ref, m_ref, wb_ref, wc_ref, wv_ref, gb_ref, p_ref):
    m = m_ref[0]
    h = _modulate(x_ref[...], g_ref[...], m[0:1], m[1:2]).astype(BF16)
    gb_ref[...] = _dot(h, wb_ref[...]).astype(gb_ref.dtype)
    p_ref[...] = (_dot(h, wc_ref[...]) * _dot(h, wv_ref[...])).astype(p_ref.dtype)


def _conv_out_kernel(gb_ref, p_ref, pp_ref, pn_ref, k_ref, cb_ref, w_ref, x_ref, m_ref, o_ref):
    i = pl.program_id(0)
    last = pl.num_programs(0) - 1
    p = p_ref[...].astype(F32)
    rows = lax.broadcasted_iota(jnp.int32, p.shape, 0)
    prev_row = jnp.where(i >= 2, pp_ref[7:8, :].astype(F32), 0.0)
    next_row = jnp.where(jnp.logical_and(i >= 1, i < last), pn_ref[0:1, :].astype(F32), 0.0)
    p_dn = jnp.where(rows == 0, prev_row, pltpu.roll(p, 1, 0))
    p_up = jnp.where(rows == TM - 1, next_row, pltpu.roll(p, TM - 1, 0))
    z = k_ref[0:1, :] * p_dn + k_ref[1:2, :] * p + k_ref[2:3, :] * p_up + cb_ref[...]
    q = (gb_ref[...].astype(F32) * z).astype(BF16)
    o_ref[...] = x_ref[...] + m_ref[0][2:3] * _dot(q, w_ref[...])


def _conv_layer(xa, mods, g1, w_in, conv_k, conv_b, w_out):
    t, d = xa.shape
    tn = 1024
    nt = d // tn
    w_in = w_in.astype(BF16)
    gb, p = pl.pallas_call(
        _conv_in_kernel,
        grid=(nt, t // TM),
        in_specs=[pl.BlockSpec((TM, d), lambda j, i: (i, 0)),
                  pl.BlockSpec((1, d), lambda j, i: (0, 0)),
                  pl.BlockSpec((1, 6, d), lambda j, i: (_mod_row(i), 0, 0)),
                  pl.BlockSpec((d, tn), lambda j, i: (0, j)),
                  pl.BlockSpec((d, tn), lambda j, i: (0, nt + j)),
                  pl.BlockSpec((d, tn), lambda j, i: (0, 2 * nt + j))],
        out_specs=[pl.BlockSpec((TM, tn), lambda j, i: (i, j))] * 2,
        out_shape=[jax.ShapeDtypeStruct((t, d), BF16)] * 2,
        compiler_params=_params(("parallel", "parallel")),
        name="conv_in",
    )(xa, g1.reshape(1, d), mods, w_in, w_in, w_in)
    r8 = TM // 8
    n8 = t // 8
    return pl.pallas_call(
        _conv_out_kernel,
        grid=(t // TM,),
        in_specs=[pl.BlockSpec((TM, d), lambda i: (i, 0)),
                  pl.BlockSpec((TM, d), lambda i: (i, 0)),
                  pl.BlockSpec((8, d), lambda i: (jnp.maximum(i * r8 - 1, 0), 0)),
                  pl.BlockSpec((8, d), lambda i: (jnp.minimum((i + 1) * r8, n8 - 1), 0)),
                  pl.BlockSpec((3, d), lambda i: (0, 0)),
                  pl.BlockSpec((1, d), lambda i: (0, 0)),
                  pl.BlockSpec((d, d), lambda i: (0, 0)),
                  pl.BlockSpec((TM, d), lambda i: (i, 0)),
                  pl.BlockSpec((1, 6, d), lambda i: (_mod_row(i), 0, 0))],
        out_specs=pl.BlockSpec((TM, d), lambda i: (i, 0)),
        out_shape=jax.ShapeDtypeStruct((t, d), F32),
        compiler_params=_params(("parallel",)),
        name="conv_out",
    )(gb, p, p, p, conv_k, conv_b.reshape(1, d), w_out.astype(BF16), xa, mods)


def _rope_tables(n_lat):
    rows = n_lat // GRID_W
    r = jnp.repeat(jnp.arange(rows, dtype=F32), GRID_W)
    col = jnp.tile(jnp.arange(GRID_W, dtype=F32), rows)
    inv = ROPE_THETA ** (-jnp.arange(ROPE_AXIS_PAIRS, dtype=F32) / ROPE_AXIS_PAIRS)
    ang = jnp.concatenate([r[:, None] * inv, col[:, None] * inv], axis=-1)
    cos, sin = jnp.cos(ang), jnp.sin(ang)
    cos_f = jnp.concatenate([cos, cos], axis=-1)
    sin_f = jnp.concatenate([-sin, sin], axis=-1)
    cos_f = jnp.concatenate([jnp.ones((CTX_LEN, HEAD_DIM), F32), cos_f], axis=0)
    sin_f = jnp.concatenate([jnp.zeros((CTX_LEN, HEAD_DIM), F32), sin_f], axis=0)
    return cos_f, sin_f


def _qkv_kernel(x_ref, g_ref, m_ref, w_ref, qg_ref, kg_ref, cos_ref, sin_ref, q_ref, k_ref, v_ref):
    m = m_ref[0]
    h = _modulate(x_ref[...], g_ref[...], m[0:1], m[1:2]).astype(BF16)
    a = _dot(h, w_ref[...])
    cos, sin = cos_ref[...], sin_ref[...]
    nq = N_HEADS * HEAD_DIM
    nk = N_KV_HEADS * HEAD_DIM

    def norm_rope(u, gain):
        ms = jnp.mean(u * u, axis=-1, keepdims=True)
        un = u * lax.rsqrt(ms + NORM_EPS) * gain
        return un * cos + pltpu.roll(un, HEAD_DIM // 2, 1) * sin

    for hh in range(N_HEADS):
        sl = slice(hh * HEAD_DIM, (hh + 1) * HEAD_DIM)
        q_ref[:, sl] = (norm_rope(a[:, sl], qg_ref[...]) * (HEAD_DIM ** -0.5 * math.log2(math.e))).astype(q_ref.dtype)
    for hh in range(N_KV_HEADS):
        sl = slice(hh * HEAD_DIM, (hh + 1) * HEAD_DIM)
        k_ref[:, sl] = norm_rope(a[:, nq + hh * HEAD_DIM:nq + (hh + 1) * HEAD_DIM], kg_ref[...]).astype(k_ref.dtype)
    ones = jnp.ones((a.shape[0], HEAD_DIM), v_ref.dtype)
    for hh in range(N_KV_HEADS):
        v_ref[:, 2 * hh * HEAD_DIM:(2 * hh + 1) * HEAD_DIM] = (
            a[:, nq + nk + hh * HEAD_DIM:nq + nk + (hh + 1) * HEAD_DIM].astype(v_ref.dtype))
        v_ref[:, (2 * hh + 1) * HEAD_DIM:(2 * hh + 2) * HEAD_DIM] = ones


def _attn_kernel(q_ref, k_ref, v_ref, o_ref, s_ref, m_ref, acc_ref):
    i = pl.program_id(1)
    t = k_ref.shape[0]
    nt_dims = (((1,), (1,)), ((), ()))
    m_ref[...] = jnp.full(m_ref.shape, -jnp.inf, F32)
    acc_ref[...] = jnp.zeros(acc_ref.shape, F32)

    def scores(slot, start, size):
        kc = k_ref[pl.ds(start, size), :]
        for j in range(Q_PER_KV):
            s_ref[slot, j, :, 0:size] = lax.dot_general(q_ref[:, j * HEAD_DIM:(j + 1) * HEAD_DIM], kc, nt_dims,
                                                        preferred_element_type=F32)

    def update(slot, start, size):
        vc = v_ref[pl.ds(start, size), :]
        for j in range(Q_PER_KV):
            s = s_ref[slot, j, :, 0:size]
            m_old = m_ref[j]
            m_new = jnp.maximum(m_old, jnp.max(s, axis=-1, keepdims=True))
            alpha = jnp.exp2(m_old - m_new)
            p = jnp.exp2((s - jnp.tile(m_new, (1, size // LANES))).astype(BF16))
            acc_ref[j] = jnp.tile(alpha, (1, 2)) * acc_ref[j] + _dot(p, vc)
            m_ref[j] = m_new

    scores(0, 0, CTX_LEN)
    update(0, 0, CTX_LEN)

    @pl.when(i > 0)
    def _():
        n_pairs = (t - CTX_LEN) // (2 * ATT_TK)
        scores(0, CTX_LEN, ATT_TK)

        def body(c, carry):
            base = pl.multiple_of(CTX_LEN + 2 * c * ATT_TK, CTX_LEN)
            scores(1, base + ATT_TK, ATT_TK)
            update(0, base, ATT_TK)
            nxt = pl.multiple_of(jnp.minimum(base + 2 * ATT_TK, t - ATT_TK), CTX_LEN)
            scores(0, nxt, ATT_TK)
            update(1, base + ATT_TK, ATT_TK)
            return carry

        lax.fori_loop(0, n_pairs, body, 0)

    for j in range(Q_PER_KV):
        acc = acc_ref[j]
        o_ref[:, j * HEAD_DIM:(j + 1) * HEAD_DIM] = (acc[:, :HEAD_DIM] / acc[:, HEAD_DIM:]).astype(o_ref.dtype)


def _attn_layer(xa, mods, g1, wq, wk, wv, wo, q_g, k_g):
    t, d = xa.shape
    nq, nk = N_HEADS * HEAD_DIM, N_KV_HEADS * HEAD_DIM
    w = jnp.concatenate([wq, wk, wv], axis=1).astype(BF16)
    cos_f, sin_f = _rope_tables(t - CTX_LEN)
    q, k, v = pl.pallas_call(
        _qkv_kernel,
        grid=(t // TM,),
        in_specs=[pl.BlockSpec((TM, d), lambda i: (i, 0)),
                  pl.BlockSpec((1, d), lambda i: (0, 0)),
                  pl.BlockSpec((1, 6, d), lambda i: (_mod_row(i), 0, 0)),
                  pl.BlockSpec((d, nq + 2 * nk), lambda i: (0, 0)),
                  pl.BlockSpec((1, HEAD_DIM), lambda i: (0, 0)),
                  pl.BlockSpec((1, HEAD_DIM), lambda i: (0, 0)),
                  pl.BlockSpec((TM, HEAD_DIM), lambda i: (i, 0)),
                  pl.BlockSpec((TM, HEAD_DIM), lambda i: (i, 0))],
        out_specs=[pl.BlockSpec((TM, nq), lambda i: (i, 0)),
                   pl.BlockSpec((TM, nk), lambda i: (i, 0)),
                   pl.BlockSpec((TM, 2 * nk), lambda i: (i, 0))],
        out_shape=[jax.ShapeDtypeStruct((t, nq), BF16),
                   jax.ShapeDtypeStruct((t, nk), BF16),
                   jax.ShapeDtypeStruct((t, 2 * nk), BF16)],
        compiler_params=_params(("parallel",)),
        name="qkv",
    )(xa, g1.reshape(1, d), mods, w, q_g.reshape(1, HEAD_DIM), k_g.reshape(1, HEAD_DIM), cos_f, sin_f)
    qw = Q_PER_KV * HEAD_DIM
    o = pl.pallas_call(
        _attn_kernel,
        grid=(N_KV_HEADS, t // TM),
        in_specs=[pl.BlockSpec((TM, qw), lambda n, i: (i, n)),
                  pl.BlockSpec((t, HEAD_DIM), lambda n, i: (0, n)),
                  pl.BlockSpec((t, 2 * HEAD_DIM), lambda n, i: (0, n))],
        out_specs=pl.BlockSpec((TM, qw), lambda n, i: (i, n)),
        out_shape=jax.ShapeDtypeStruct((t, nq), BF16),
        scratch_shapes=[pltpu.VMEM((2, Q_PER_KV, TM, ATT_TK), F32),
                        pltpu.VMEM((Q_PER_KV, TM, LANES), F32),
                        pltpu.VMEM((Q_PER_KV, TM, 2 * HEAD_DIM), F32)],
        compiler_params=_params(("parallel", "parallel")),
        name="attention",
    )(q, k, v)
    return _mm_resid(o, wo.astype(BF16), jnp.zeros((d,), F32), xa, mods, 2)


def _s5_matrices(a_re, a_im, log_dt, b_re, b_im, c_re, c_im, reverse):
    l, g, p, c = SSM_SUB, SSM_GROUPS, SSM_STATE, SSM_GROUP_DIM
    a = lax.complex(a_re.astype(F32), a_im.astype(F32))
    adt = a * jnp.exp(log_dt.astype(F32))[:, None]
    a_bar = jnp.exp(adt)
    b_bar = ((a_bar - 1) / a)[..., None] * lax.complex(b_re.astype(F32), b_im.astype(F32))
    c_mat = lax.complex(c_re.astype(F32), c_im.astype(F32))
    pw = jnp.exp(adt[:, None, :] * jnp.arange(l + 1, dtype=F32)[None, :, None])
    kern = jnp.einsum('gcp,gkp,gpd->gkcd', c_mat, pw[:, :l], b_bar,
                      precision=lax.Precision.HIGHEST).real
    e = pw[:, l - 1 - jnp.arange(l), :][:, :, None, :] * jnp.transpose(b_bar, (0, 2, 1))[:, None, :, :]
    cp = jnp.transpose(c_mat[:, None, :, :] * pw[:, 1:, None, :], (0, 3, 1, 2))

    nb, gpb = SSM_CB, SSM_GPB
    ch_grp = jnp.arange(LANES) // c
    st_grp = jnp.arange(gpb * p) // p
    kfull = jnp.transpose(kern, (0, 1, 3, 2)).reshape(nb, gpb, l, c, c)
    kfull = jnp.transpose(kfull, (0, 2, 1, 3, 4)).reshape(nb, l, LANES, c)
    kfull = jnp.tile(kfull, (1, 1, 1, gpb)) * (ch_grp[:, None] == ch_grp[None, :])
    lag = jnp.arange(l)[None, :] - jnp.arange(l)[:, None]
    m = jnp.where((lag >= 0)[None, :, :, None, None], kfull[:, jnp.clip(lag, 0, l - 1)], 0.0)

    def widen_e(x):
        x = jnp.transpose(x.reshape(nb, gpb, l, c, p), (0, 2, 1, 3, 4)).reshape(nb, l, LANES, p)
        return jnp.tile(x, (1, 1, 1, gpb)) * (ch_grp[:, None] == st_grp[None, :])

    def widen_c(x):
        x = x.reshape(nb, gpb * p, l, c)
        return jnp.tile(x, (1, 1, 1, gpb)) * (st_grp[:, None, None] == ch_grp[None, None, :])

    e_re, e_im = widen_e(e.real), widen_e(e.imag)
    cp_re, cp_im = widen_c(cp.real), widen_c(-cp.imag)
    if reverse:
        m = jnp.flip(m, (1, 2))
        e_re, e_im = jnp.flip(e_re, 1), jnp.flip(e_im, 1)
        cp_re, cp_im = jnp.flip(cp_re, 2), jnp.flip(cp_im, 2)
    mbig = jnp.transpose(m.astype(BF16), (0, 1, 3, 2, 4)).reshape(nb, l * LANES, l * LANES)
    w1 = jnp.concatenate([e_re, e_im], axis=3).astype(BF16).reshape(nb, l * LANES, 2 * gpb * p)
    w2 = jnp.concatenate([mbig, cp_re.astype(BF16).reshape(nb, gpb * p, l * LANES),
                          cp_im.astype(BF16).reshape(nb, gpb * p, l * LANES)], axis=1)
    al = pw[:, l].reshape(SSM_CB, 1, SSM_GPB * p)
    return w1, w2, al.real, al.imag


def _s5_scan_kernel(*refs, reverse):
    l = SSM_SUB
    u_refs = refs[:l]
    w1_ref, w2_ref, ar_ref, ai_ref, h0_ref, y_ref, hfin_ref, v_ref, h_ref = refs[l:]
    ns = ar_ref.shape[-1]
    rows = v_ref.shape[0]

    @pl.when(pl.program_id(1) == 0)
    def _():
        h_ref[...] = h0_ref[0]

    u = jnp.concatenate([r[...] for r in u_refs], axis=1)
    v_ref[...] = _dot(u, w1_ref[0])
    ar, ai = ar_ref[0], ai_ref[0]

    def body(step, carry):
        hr, hi = carry
        j = rows - 1 - step if reverse else step
        vr = v_ref[pl.ds(j, 1), 0:ns]
        vi = v_ref[pl.ds(j, 1), ns:2 * ns]
        v_ref[pl.ds(j, 1), 0:ns] = hr
        v_ref[pl.ds(j, 1), ns:2 * ns] = hi
        return ar * hr - ai * hi + vr, ar * hi + ai * hr + vi

    hr, hi = lax.fori_loop(0, rows, body, (h_ref[0:1, :], h_ref[1:2, :]))
    h_ref[0:1, :] = hr
    h_ref[1:2, :] = hi
    hfin_ref[0] = h_ref[...]
    y = _dot(jnp.concatenate([u, v_ref[...].astype(BF16)], axis=1), w2_ref[0])
    for t in range(l):
        y_ref[t] = y[:, t * LANES:(t + 1) * LANES]


def _s5_scan(seq, mats, h0, reverse):
    t, d = seq.shape
    l = SSM_SUB
    w1, w2, al_re, al_im = mats
    n_sub = t // l
    rows = max(r for r in range(8, min(n_sub, 512) + 1, 8) if n_sub % r == 0)
    nr = n_sub // rows
    ns = SSM_GPB * SSM_STATE
    u2 = seq.reshape(n_sub, l * d)
    row_blk = (lambda r: nr - 1 - r) if reverse else (lambda r: r)
    blk = lambda s: pl.BlockSpec((rows, LANES), lambda cb, r, s=s: (row_blk(r), s * SSM_CB + cb))
    return pl.pallas_call(
        functools.partial(_s5_scan_kernel, reverse=reverse),
        grid=(SSM_CB, nr),
        in_specs=[blk(s) for s in range(l)] + [
            pl.BlockSpec((1,) + w1.shape[1:], lambda cb, r: (cb, 0, 0)),
            pl.BlockSpec((1,) + w2.shape[1:], lambda cb, r: (cb, 0, 0)),
            pl.BlockSpec((1, 1, ns), lambda cb, r: (cb, 0, 0)),
            pl.BlockSpec((1, 1, ns), lambda cb, r: (cb, 0, 0)),
            pl.BlockSpec((1, 2, ns), lambda cb, r: (cb, 0, 0))],
        out_specs=[pl.BlockSpec((l, rows, LANES), lambda cb, r: (0, row_blk(r), cb)),
                   pl.BlockSpec((1, 2, ns), lambda cb, r: (cb, 0, 0))],
        out_shape=[jax.ShapeDtypeStruct((l, n_sub, d), F32),
                   jax.ShapeDtypeStruct((SSM_CB, 2, ns), F32)],
        scratch_shapes=[pltpu.VMEM((rows, 2 * ns), F32), pltpu.VMEM((2, ns), F32)],
        compiler_params=_params(("parallel", "arbitrary")),
        name="s5_scan",
    )(*([u2] * l), w1, w2, al_re, al_im, h0)


def _gelu_tanh(x):
    return 0.5 * x * (1.0 + jnp.tanh(math.sqrt(2.0 / math.pi) * (x + 0.044715 * x * x * x)))


def _s5_glu_kernel(x_ref, g_ref, m_ref, d_ref, yf_ref, yb_ref, wa_ref, wg_ref, ba_ref, bg_ref, xo_ref, mo_ref, o_ref):
    m = m_ref[0]
    h = _modulate(x_ref[...], g_ref[...], m[0:1], m[1:2])
    z = _gelu_tanh(d_ref[...] * h + yf_ref[0] + yb_ref[0]).astype(BF16)
    a = _dot(z, wa_ref[...]) + ba_ref[...]
    g = _dot(z, wg_ref[...]) + bg_ref[...]
    o_ref[...] = xo_ref[...] + mo_ref[0][2:3] * (a * jax.nn.sigmoid(g))


def _s5_layer(xa, mods, g1, a_re, a_im, log_dt, b_re, b_im, c_re, c_im, d_skip, glu_w, glu_b):
    t, d = xa.shape
    l = SSM_SUB
    h = _prep(xa, g1, mods, 0)
    h_ctx, h_lat = h[:CTX_LEN], h[CTX_LEN:]
    x_lat = xa[CTX_LEN:]
    ns = SSM_GPB * SSM_STATE
    ys = []
    for direction in range(2):
        rev = direction == 1
        mats = _s5_matrices(a_re[direction], a_im[direction], log_dt[direction], b_re[direction], b_im[direction],
                            c_re[direction], c_im[direction], rev)
        _, state = _s5_scan(h_ctx, mats, jnp.zeros((SSM_CB, 2, ns), F32), rev)
        y, _ = _s5_scan(h_lat, mats, state, rev)
        ys.append(y)
    tn = 1024
    nt = d // tn
    n_sub = (t - CTX_LEN) // l
    assert n_sub % TM == 0
    glu_w = glu_w.astype(BF16)
    out = pl.pallas_call(
        _s5_glu_kernel,
        grid=(nt, n_sub // TM, l),
        in_specs=[pl.BlockSpec((TM, d), lambda j, i, s: (i, s)),
                  pl.BlockSpec((1, d), lambda j, i, s: (0, 0)),
                  pl.BlockSpec((1, 6, d), lambda j, i, s: (1, 0, 0)),
                  pl.BlockSpec((1, d), lambda j, i, s: (0, 0)),
                  pl.BlockSpec((1, TM, d), lambda j, i, s: (s, i, 0)),
                  pl.BlockSpec((1, TM, d), lambda j, i, s: (s, i, 0)),
                  pl.BlockSpec((d, tn), lambda j, i, s: (0, j)),
                  pl.BlockSpec((d, tn), lambda j, i, s: (0, nt + j)),
                  pl.BlockSpec((1, tn), lambda j, i, s: (0, j)),
                  pl.BlockSpec((1, tn), lambda j, i, s: (0, nt + j)),
                  pl.BlockSpec((TM, tn), lambda j, i, s: (i, s * nt + j)),
                  pl.BlockSpec((1, 6, tn), lambda j, i, s: (1, 0, j))],
        out_specs=pl.BlockSpec((TM, tn), lambda j, i, s: (i, s * nt + j)),
        out_shape=jax.ShapeDtypeStruct((n_sub, l * d), F32),
        compiler_params=_params(("parallel", "parallel", "parallel")),
        name="s5_glu",
    )(x_lat.reshape(n_sub, l * d), g1.reshape(1, d), mods, d_skip.reshape(1, d), ys[0], ys[1], glu_w, glu_w,
      glu_b.reshape(1, 2 * d), glu_b.reshape(1, 2 * d), x_lat.reshape(n_sub, l * d), mods)
    return out.reshape(t - CTX_LEN, d)


def _router_kernel(x_ref, g_ref, m_ref, wr_ref, br_ref, tri_ref, h_ref, idx_ref, gate_ref, rank_ref, cnt_ref, base_ref):
    i = pl.program_id(0)

    @pl.when(i == 0)
    def _():
        base_ref[...] = jnp.zeros(base_ref.shape, F32)

    m = m_ref[0]
    h = _modulate(x_ref[...], g_ref[...], m[3:4], m[4:5])
    h_ref[...] = h.astype(h_ref.dtype)
    logits = lax.dot_general(wr_ref[...], h, (((1,), (1,)), ((), ())), precision=lax.Precision.HIGHEST,
                             preferred_element_type=F32) + br_ref[...]
    e_iota = lax.broadcasted_iota(jnp.int32, logits.shape, 0).astype(F32)
    base = base_ref[...]
    tops, idxs, ranks = [], [], []
    for _ in range(TOP_K):
        top = jnp.max(logits, axis=0, keepdims=True)
        idx = jnp.min(jnp.where(logits == top, e_iota, float(N_EXPERTS)), axis=0, keepdims=True)
        sel = e_iota == idx
        logits = jnp.where(sel, -jnp.inf, logits)
        onehot = sel.astype(BF16)
        before = _dot(onehot, tri_ref[...])
        ranks.append(jnp.sum(jnp.where(sel, base + before, 0.0), axis=0, keepdims=True))
        base = base + jnp.sum(sel.astype(F32), axis=1, keepdims=True)
        tops.append(top)
        idxs.append(idx)
    base_ref[...] = base
    cnt_ref[...] = jnp.broadcast_to(base, cnt_ref.shape)
    ex = [jnp.exp(v - tops[0]) for v in tops]
    den = ex[0] + ex[1] + ex[2] + ex[3]
    idx_ref[...] = jnp.concatenate(idxs, axis=0).astype(jnp.int32)
    gate_ref[...] = jnp.concatenate([v / den for v in ex], axis=0)
    rank_ref[...] = jnp.concatenate(ranks, axis=0).astype(jnp.int32)


def _ffn_kernel(be_ref, x_ref, w1_ref, b1_ref, w2_ref, b2_ref, o_ref, w1b_ref, w2b_ref):
    i = pl.program_id(0)
    new_expert = jnp.logical_or(i == 0, be_ref[0, i] != be_ref[0, jnp.maximum(i - 1, 0)])

    @pl.when(new_expert)
    def _():
        def cast(ref, dst):
            n_chunks = ref.shape[2] // WCAST_ROWS

            def body(r, carry):
                rows = pl.ds(pl.multiple_of(r * WCAST_ROWS, WCAST_ROWS), WCAST_ROWS)
                dst[rows, :] = ref[0, 0, rows, :].astype(BF16)
                return carry

            lax.fori_loop(0, n_chunks, body, 0)

        cast(w1_ref, w1b_ref)
        cast(w2_ref, w2b_ref)

    @pl.when(be_ref[1, i] > 0)
    def _():
        h = _dot(x_ref[...], w1b_ref[...]) + b1_ref[0]
        gate = jnp.minimum(h[:, :D_EXPERT], SWIGLU_LIMIT)
        lin = jnp.clip(h[:, D_EXPERT:], -SWIGLU_LIMIT, SWIGLU_LIMIT)
        act = ((lin + 1.0) * gate * jax.nn.sigmoid(SWIGLU_ALPHA * gate)).astype(BF16)
        o_ref[...] = (_dot(act, w2b_ref[...]) + b2_ref[0]).astype(o_ref.dtype)


def _combine_kernel(y_ref, gt_ref, x_ref, m_ref, o_ref):
    gt = gt_ref[...]
    acc = gt[:, 0:1] * y_ref[0].astype(F32)
    for k in range(1, TOP_K):
        acc = acc + gt[:, k:k + 1] * y_ref[k].astype(F32)
    o_ref[...] = x_ref[...] + m_ref[0][5:6] * acc


def _moe_layer(xa, mods, g2, w_r, b_r, w1, b1, w2, b2, layer):
    t, d = xa.shape
    nb = t // TM
    tri = jnp.asarray(np.triu(np.ones((TM, TM), np.float32), 1), BF16)
    h2, idx, gates, rank, cnt = pl.pallas_call(
        _router_kernel,
        grid=(nb,),
        in_specs=[pl.BlockSpec((TM, d), lambda i: (i, 0)),
                  pl.BlockSpec((1, d), lambda i: (0, 0)),
                  pl.BlockSpec((1, 6, d), lambda i: (_mod_row(i), 0, 0)),
                  pl.BlockSpec((N_EXPERTS, d), lambda i: (0, 0)),
                  pl.BlockSpec((N_EXPERTS, 1), lambda i: (0, 0)),
                  pl.BlockSpec((TM, TM), lambda i: (0, 0))],
        out_specs=[pl.BlockSpec((TM, d), lambda i: (i, 0)),
                   pl.BlockSpec((TOP_K, TM), lambda i: (0, i)),
                   pl.BlockSpec((TOP_K, TM), lambda i: (0, i)),
                   pl.BlockSpec((TOP_K, TM), lambda i: (0, i)),
                   pl.BlockSpec((N_EXPERTS, LANES), lambda i: (0, 0))],
        out_shape=[jax.ShapeDtypeStruct((t, d), BF16),
                   jax.ShapeDtypeStruct((TOP_K, t), jnp.int32),
                   jax.ShapeDtypeStruct((TOP_K, t), F32),
                   jax.ShapeDtypeStruct((TOP_K, t), jnp.int32),
                   jax.ShapeDtypeStruct((N_EXPERTS, LANES), F32)],
        scratch_shapes=[pltpu.VMEM((N_EXPERTS, 1), F32)],
        compiler_params=_params(("arbitrary",)),
        name="router",
    )(xa, g2.reshape(1, d), mods, w_r.T, b_r.reshape(N_EXPERTS, 1), tri)

    counts = cnt[:, 0].astype(jnp.int32)
    padded = (counts + MOE_TM - 1) // MOE_TM * MOE_TM
    pend = jnp.cumsum(padded)
    pstart = pend - padded
    dest = pstart[idx] + rank
    n_rows = -(-t * TOP_K // MOE_TM) * MOE_TM + N_EXPERTS * MOE_TM
    n_blocks = n_rows // MOE_TM
    blk_start = jnp.arange(n_blocks, dtype=jnp.int32) * MOE_TM
    blk_used = (blk_start < pend[-1]).astype(jnp.int32)
    blk_e = jnp.minimum(jnp.sum((pend[None, :] <= blk_start[:, None]).astype(jnp.int32), axis=1), N_EXPERTS - 1)
    last_e = blk_e[jnp.maximum(pend[-1] // MOE_TM - 1, 0)]
    blk_e = jnp.where(blk_used > 0, blk_e, last_e)
    be = jnp.stack([blk_e, blk_used])
    tok = jnp.broadcast_to(jnp.arange(t, dtype=jnp.int32)[None, :], (TOP_K, t))
    row_tok = (jnp.arange(n_rows, dtype=jnp.int32) % t).at[dest.reshape(-1)].set(
        tok.reshape(-1), unique_indices=True, mode='promise_in_bounds')
    xs = h2.at[row_tok].get(mode='promise_in_bounds')

    ys = pl.pallas_call(
        _ffn_kernel,
        grid_spec=pltpu.PrefetchScalarGridSpec(
            num_scalar_prefetch=1,
            grid=(n_blocks,),
            in_specs=[pl.BlockSpec((MOE_TM, d), lambda i, be: (i, 0)),
                      pl.BlockSpec((1, 1, d, 2 * D_EXPERT), lambda i, be: (layer, be[0, i], 0, 0),
                                   pipeline_mode=pl.Buffered(1)),
                      pl.BlockSpec((1, 1, 2 * D_EXPERT), lambda i, be: (be[0, i], 0, 0)),
                      pl.BlockSpec((1, 1, D_EXPERT, d), lambda i, be: (layer, be[0, i], 0, 0),
                                   pipeline_mode=pl.Buffered(1)),
                      pl.BlockSpec((1, 1, d), lambda i, be: (be[0, i], 0, 0))],
            out_specs=pl.BlockSpec((MOE_TM, d), lambda i, be: (i, 0)),
            scratch_shapes=[pltpu.VMEM((d, 2 * D_EXPERT), BF16), pltpu.VMEM((D_EXPERT, d), BF16)]),
        out_shape=jax.ShapeDtypeStruct((n_rows, d), BF16),
        compiler_params=_params(("arbitrary",)),
        name="moe_ffn",
    )(be, xs, w1, b1.reshape(N_EXPERTS, 1, 2 * D_EXPERT), w2, b2.reshape(N_EXPERTS, 1, d))

    y4 = ys.at[dest.reshape(-1)].get(mode='promise_in_bounds', unique_indices=True).reshape(TOP_K, t, d)
    return pl.pallas_call(
        _combine_kernel,
        grid=(nb,),
        in_specs=[pl.BlockSpec((TOP_K, TM, d), lambda i: (0, i, 0)),
                  pl.BlockSpec((TM, TOP_K), lambda i: (i, 0)),
                  pl.BlockSpec((TM, d), lambda i: (i, 0)),
                  pl.BlockSpec((1, 6, d), lambda i: (_mod_row(i), 0, 0))],
        out_specs=pl.BlockSpec((TM, d), lambda i: (i, 0)),
        out_shape=jax.ShapeDtypeStruct((t, d), F32),
        compiler_params=_params(("parallel",)),
        name="moe_combine",
    )(y4, gates.T, xa, mods)


def kernel(x, c, ctx, c_ctx, ada_w, ada_b, norm1_g, norm2_g, fourier_w_out, fourier_b_out, conv_w_in, conv_k, conv_b, conv_w_out, attn_wq, attn_wk, attn_wv, attn_wo, attn_q_g, attn_k_g, ssm_a_re, ssm_a_im, ssm_log_dt, ssm_b_re, ssm_b_im, ssm_c_re, ssm_c_im, ssm_d, ssm_glu_w, ssm_glu_b, router_w, router_b, moe_w1, moe_b1, moe_w2, moe_b2):
    bsz, n, d = x.shape
    assert bsz == 1 and d == D_MODEL and ctx.shape[1] == CTX_LEN and n % TM == 0
    depth = ada_w.shape[0]
    cond8 = jnp.zeros((8, d), F32).at[0].set(c_ctx).at[1].set(c[0])
    mods = _ada_all(cond8, ada_w, ada_b).reshape(depth, 8, 6, d)
    assert depth == 4
    xa = jnp.concatenate([ctx[0], x[0]], axis=0)
    for i in range(depth):
        m, j = i % 4, i // 4
        mods_i = mods[i]
        if m == 0:
            xa = _fourier_layer(xa, mods[i], norm1_g[i], fourier_w_out[j], fourier_b_out[j])
        elif m == 1:
            xa = _conv_layer(xa, mods[i], norm1_g[i], conv_w_in[j], conv_k[j], conv_b[j], conv_w_out[j])
        elif m == 2:
            xa = _attn_layer(xa, mods[i], norm1_g[i], attn_wq[j], attn_wk[j], attn_wv[j], attn_wo[j],
                             attn_q_g[j], attn_k_g[j])
        else:
            xa = _s5_layer(xa, mods[i], norm1_g[i], ssm_a_re[j], ssm_a_im[j], ssm_log_dt[j], ssm_b_re[j],
                           ssm_b_im[j], ssm_c_re[j], ssm_c_im[j], ssm_d[j], ssm_glu_w[j], ssm_glu_b[j])
            mods_i = mods_i.at[0].set(mods_i[1])
        xa = _moe_layer(xa, mods_i, norm2_g[i], router_w[i], router_b[i], moe_w1, moe_b1[i], moe_w2, moe_b2[i], i)
    return xa[None]
```

```python
import functools
import math

import numpy as np
import jax
import jax.numpy as jnp
from jax import lax
from jax.experimental import pallas as pl
from jax.experimental.pallas import tpu as pltpu

F32 = jnp.float32
BF16 = jnp.bfloat16

D_MODEL = 2048
GRID_W = 64
CTX_LEN = 256
FOURIER_GROUPS = 8
FOURIER_GROUP_DIM = D_MODEL // FOURIER_GROUPS
HEAD_DIM = 128
N_HEADS = D_MODEL // HEAD_DIM
N_KV_HEADS = 4
Q_PER_KV = N_HEADS // N_KV_HEADS
ROPE_AXIS_PAIRS = HEAD_DIM // 4
ROPE_THETA = 10000.0
SSM_GROUP_DIM = 16
SSM_GROUPS = D_MODEL // SSM_GROUP_DIM
SSM_STATE = 64
N_EXPERTS = 32
TOP_K = 4
D_EXPERT = 1024
SWIGLU_LIMIT = 7.0
SWIGLU_ALPHA = 1.702
NORM_EPS = 1e-6

LANES = 128
TM = 256
DFT_N2 = 128
SSM_SUB = 8
SSM_CB = D_MODEL // LANES
SSM_GPB = LANES // SSM_GROUP_DIM
MOE_TM = 512
ATT_TK = 1024
WCAST_ROWS = 256
VMEM_LIMIT = 56 * 2 ** 20


def _params(sem, vmem=VMEM_LIMIT):
    return pltpu.CompilerParams(dimension_semantics=sem, vmem_limit_bytes=vmem)


def _mod_row(i):
    return jnp.minimum(i, 1)


def _modulate(x, g, shift, scale):
    ms = jnp.mean(x * x, axis=-1, keepdims=True)
    return (x * lax.rsqrt(ms + NORM_EPS) * g) * (1.0 + scale) + shift


def _dot(a, b):
    return jnp.dot(a, b, preferred_element_type=F32)


def _ada_kernel(c_ref, w_ref, b_ref, o_ref):
    c = c_ref[...]
    s = c * jax.nn.sigmoid(c)
    o_ref[0] = jnp.dot(s, w_ref[0], precision=lax.Precision.HIGHEST, preferred_element_type=F32) + b_ref[0]


def _ada_all(cond8, ada_w, ada_b):
    depth, d, n6 = ada_w.shape
    tn = 1024
    return pl.pallas_call(
        _ada_kernel,
        grid=(depth, n6 // tn),
        in_specs=[pl.BlockSpec((8, d), lambda l, j: (0, 0)),
                  pl.BlockSpec((1, d, tn), lambda l, j: (l, 0, j)),
                  pl.BlockSpec((1, 1, tn), lambda l, j: (l, 0, j))],
        out_specs=pl.BlockSpec((1, 8, tn), lambda l, j: (l, 0, j)),
        out_shape=jax.ShapeDtypeStruct((depth, 8, n6), F32),
        compiler_params=_params(("parallel", "parallel")),
        name="ada",
    )(cond8, ada_w, ada_b.reshape(depth, 1, n6))


def _prep_kernel(x_ref, g_ref, m_ref, o_ref, *, which):
    m = m_ref[0]
    h = _modulate(x_ref[...], g_ref[...], m[3 * which:3 * which + 1], m[3 * which + 1:3 * which + 2])
    o_ref[...] = h.astype(o_ref.dtype)


def _prep(xa, g, mods, which):
    t, d = xa.shape
    return pl.pallas_call(
        functools.partial(_prep_kernel, which=which),
        grid=(t // TM,),
        in_specs=[pl.BlockSpec((TM, d), lambda i: (i, 0)),
                  pl.BlockSpec((1, d), lambda i: (0, 0)),
                  pl.BlockSpec((1, 6, d), lambda i: (_mod_row(i), 0, 0))],
        out_specs=pl.BlockSpec((TM, d), lambda i: (i, 0)),
        out_shape=jax.ShapeDtypeStruct((t, d), BF16),
        compiler_params=_params(("parallel",)),
        name="prep",
    )(xa, g.reshape(1, d), mods)


def _mm_resid_kernel(a_ref, w_ref, b_ref, x_ref, m_ref, o_ref, *, gate_row):
    y = _dot(a_ref[...], w_ref[...]) + b_ref[...]
    o_ref[...] = x_ref[...] + m_ref[0][gate_row:gate_row + 1] * y


def _mm_resid(a, w, b, xa, mods, gate_row):
    t, d = xa.shape
    k = a.shape[1]
    return pl.pallas_call(
        functools.partial(_mm_resid_kernel, gate_row=gate_row),
        grid=(t // TM,),
        in_specs=[pl.BlockSpec((TM, k), lambda i: (i, 0)),
                  pl.BlockSpec((k, d), lambda i: (0, 0)),
                  pl.BlockSpec((1, d), lambda i: (0, 0)),
                  pl.BlockSpec((TM, d), lambda i: (i, 0)),
                  pl.BlockSpec((1, 6, d), lambda i: (_mod_row(i), 0, 0))],
        out_specs=pl.BlockSpec((TM, d), lambda i: (i, 0)),
        out_shape=jax.ShapeDtypeStruct((t, d), F32),
        compiler_params=_params(("parallel",)),
        name="mm_resid",
    )(a, w, b.reshape(1, d), xa, mods)


def _dft_tables(n1, n2):
    n = n1 * n2
    k1 = np.arange(n1, dtype=np.float64)
    t = (np.arange(n1, dtype=np.float64)[None, :] * n2 + np.arange(n2, dtype=np.float64)[:, None])
    ang = 2.0 * np.pi * k1[None, :, None] * t[:, None, :] / n
    g = np.concatenate([np.cos(ang), -np.sin(ang)], axis=1) / math.sqrt(n1)
    a2 = 2.0 * np.pi * np.outer(np.arange(n2), np.arange(n2)) / n2
    c2, s2 = np.cos(a2) / math.sqrt(n2), np.sin(a2) / math.sqrt(n2)
    f2 = np.block([[c2, s2], [-s2, c2]])
    return jnp.asarray(g, BF16), jnp.asarray(f2, BF16)


def _dft_stage1_kernel(x_ref, g_ref, re_ref, im_ref):
    n1 = x_ref.shape[0]
    a = _dot(g_ref[0], x_ref[...])
    re_ref[0] = a[:n1].astype(re_ref.dtype)
    im_ref[0] = a[n1:].astype(im_ref.dtype)


def _dft_stage2_kernel(re_ref, im_ref, f_ref, zre_ref, zim_ref):
    n2 = re_ref.shape[0]
    z = _dot(f_ref[...], jnp.concatenate([re_ref[...], im_ref[...]], axis=0))
    zre_ref[...] = z[:n2].astype(zre_ref.dtype)
    zim_ref[...] = z[n2:].astype(zim_ref.dtype)


def _pos_dft(h):
    n, d = h.shape
    if n <= 512:
        n1, n2 = n, 1
    else:
        n1, n2 = n // DFT_N2, DFT_N2
    g, f2 = _dft_tables(n1, n2)
    are, aim = pl.pallas_call(
        _dft_stage1_kernel,
        grid=(n2,),
        in_specs=[pl.BlockSpec((n1, d), lambda j: (0, j)),
                  pl.BlockSpec((1, 2 * n1, n1), lambda j: (j, 0, 0))],
        out_specs=[pl.BlockSpec((1, n1, d), lambda j: (j, 0, 0))] * 2,
        out_shape=[jax.ShapeDtypeStruct((n2, n1, d), BF16)] * 2,
        compiler_params=_params(("parallel",)),
        name="dft_stage1",
    )(h.reshape(n1, n2 * d), g)
    if n2 == 1:
        return are.reshape(n, d), aim.reshape(n, d)
    tn = 2048
    zre, zim = pl.pallas_call(
        _dft_stage2_kernel,
        grid=(n1 * d // tn,),
        in_specs=[pl.BlockSpec((n2, tn), lambda j: (0, j)),
                  pl.BlockSpec((n2, tn), lambda j: (0, j)),
                  pl.BlockSpec((2 * n2, 2 * n2), lambda j: (0, 0))],
        out_specs=[pl.BlockSpec((n2, tn), lambda j: (0, j))] * 2,
        out_shape=[jax.ShapeDtypeStruct((n2, n1 * d), BF16)] * 2,
        compiler_params=_params(("parallel",)),
        name="dft_stage2",
    )(are.reshape(n2, n1 * d), aim.reshape(n2, n1 * d), f2)
    return zre.reshape(n, d), zim.reshape(n, d)


def _fourier_out_kernel(zrc_ref, zic_ref, zrl_ref, zil_ref, cc_ref, sc_ref, w_ref, b_ref, x_ref, m_ref, o_ref):
    is_ctx = pl.program_id(0) == 0
    zre = jnp.where(is_ctx, zrc_ref[...], zrl_ref[...])
    zim = jnp.where(is_ctx, zic_ref[...], zil_ref[...])
    gd = FOURIER_GROUP_DIM
    parts = []
    for g in range(FOURIER_GROUPS):
        sl = slice(g * gd, (g + 1) * gd)
        parts.append(_dot(zre[:, sl], cc_ref[...]) + _dot(zim[:, sl], sc_ref[...]))
    f = jnp.concatenate(parts, axis=1).astype(BF16)
    y = _dot(f, w_ref[...]) + b_ref[...]
    o_ref[...] = x_ref[...] + m_ref[0][2:3] * y


def _fourier_layer(xa, mods, g1, w_out, b_out):
    t, d = xa.shape
    h = _prep(xa, g1, mods, 0)
    zrc, zic = _pos_dft(h[:CTX_LEN])
    zrl, zil = _pos_dft(h[CTX_LEN:])
    gd = FOURIER_GROUP_DIM
    ang = 2.0 * np.pi * np.outer(np.arange(gd), np.arange(gd)) / gd
    cc = jnp.asarray(np.cos(ang) / math.sqrt(gd), BF16)
    sc = jnp.asarray(np.sin(ang) / math.sqrt(gd), BF16)
    lat = lambda i: (jnp.maximum(i - 1, 0), 0)
    return pl.pallas_call(
        _fourier_out_kernel,
        grid=(t // TM,),
        in_specs=[pl.BlockSpec((TM, d), lambda i: (0, 0)),
                  pl.BlockSpec((TM, d), lambda i: (0, 0)),
                  pl.BlockSpec((TM, d), lat),
                  pl.BlockSpec((TM, d), lat),
                  pl.BlockSpec((gd, gd), lambda i: (0, 0)),
                  pl.BlockSpec((gd, gd), lambda i: (0, 0)),
                  pl.BlockSpec((d, d), lambda i: (0, 0)),
                  pl.BlockSpec((1, d), lambda i: (0, 0)),
                  pl.BlockSpec((TM, d), lambda i: (i, 0)),
                  pl.BlockSpec((1, 6, d), lambda i: (_mod_row(i), 0, 0))],
        out_specs=pl.BlockSpec((TM, d), lambda i: (i, 0)),
        out_shape=jax.ShapeDtypeStruct((t, d), F32),
        compiler_params=_params(("parallel",)),
        name="fourier_out",
    )(zrc, zic, zrl, zil, cc, sc, w_out.astype(BF16), b_out.reshape(1, d), xa, mods)


def _conv_in_kernel(x_ref, g_ref, m_ref, wb_ref, wc_ref, wv_ref, gb_ref, p_ref):
    m = m_ref[0]
    h = _modulate(x_ref[...], g_ref[...], m[0:1], m[1:2]).astype(BF16)
    gb_ref[...] = _dot(h, wb_ref[...]).astype(gb_ref.dtype)
    p_ref[...] = (_dot(h, wc_ref[...]) * _dot(h, wv_ref[...])).astype(p_ref.dtype)


def _conv_out_kernel(gb_ref, p_ref, pp_ref, pn_ref, k_ref, cb_ref, w_ref, x_ref, m_ref, o_ref):
    i = pl.program_id(0)
    last = pl.num_programs(0) - 1
    p = p_ref[...].astype(F32)
    rows = lax.broadcasted_iota(jnp.int32, p.shape, 0)
    prev_row = jnp.where(i >= 2, pp_ref[7:8, :].astype(F32), 0.0)
    next_row = jnp.where(jnp.logical_and(i >= 1, i < last), pn_ref[0:1, :].astype(F32), 0.0)
    p_dn = jnp.where(rows == 0, prev_row, pltpu.roll(p, 1, 0))
    p_up = jnp.where(rows == TM - 1, next_row, pltpu.roll(p, TM - 1, 0))
    z = k_ref[0:1, :] * p_dn + k_ref[1:2, :] * p + k_ref[2:3, :] * p_up + cb_ref[...]
    q = (gb_ref[...].astype(F32) * z).astype(BF16)
    o_ref[...] = x_ref[...] + m_ref[0][2:3] * _dot(q, w_ref[...])


def _conv_layer(xa, mods, g1, w_in, conv_k, conv_b, w_out):
    t, d = xa.shape
    tn = 1024
    nt = d // tn
    w_in = w_in.astype(BF16)
    gb, p = pl.pallas_call(
        _conv_in_kernel,
        grid=(nt, t // TM),
        in_specs=[pl.BlockSpec((TM, d), lambda j, i: (i, 0)),
                  pl.BlockSpec((1, d), lambda j, i: (0, 0)),
                  pl.BlockSpec((1, 6, d), lambda j, i: (_mod_row(i), 0, 0)),
                  pl.BlockSpec((d, tn), lambda j, i: (0, j)),
                  pl.BlockSpec((d, tn), lambda j, i: (0, nt + j)),
                  pl.BlockSpec((d, tn), lambda j, i: (0, 2 * nt + j))],
        out_specs=[pl.BlockSpec((TM, tn), lambda j, i: (i, j))] * 2,
        out_shape=[jax.ShapeDtypeStruct((t, d), BF16)] * 2,
        compiler_params=_params(("parallel", "parallel")),
        name="conv_in",
    )(xa, g1.reshape(1, d), mods, w_in, w_in, w_in)
    r8 = TM // 8
    n8 = t // 8
    return pl.pallas_call(
        _conv_out_kernel,
        grid=(t // TM,),
        in_specs=[pl.BlockSpec((TM, d), lambda i: (i, 0)),
                  pl.BlockSpec((TM, d), lambda i: (i, 0)),
                  pl.BlockSpec((8, d), lambda i: (jnp.maximum(i * r8 - 1, 0), 0)),
                  pl.BlockSpec((8, d), lambda i: (jnp.minimum((i + 1) * r8, n8 - 1), 0)),
                  pl.BlockSpec((3, d), lambda i: (0, 0)),
                  pl.BlockSpec((1, d), lambda i: (0, 0)),
                  pl.BlockSpec((d, d), lambda i: (0, 0)),
                  pl.BlockSpec((TM, d), lambda i: (i, 0)),
                  pl.BlockSpec((1, 6, d), lambda i: (_mod_row(i), 0, 0))],
        out_specs=pl.BlockSpec((TM, d), lambda i: (i, 0)),
        out_shape=jax.ShapeDtypeStruct((t, d), F32),
        compiler_params=_params(("parallel",)),
        name="conv_out",
    )(gb, p, p, p, conv_k, conv_b.reshape(1, d), w_out.astype(BF16), xa, mods)


def _rope_tables(n_lat):
    rows = n_lat // GRID_W
    r = jnp.repeat(jnp.arange(rows, dtype=F32), GRID_W)
    col = jnp.tile(jnp.arange(GRID_W, dtype=F32), rows)
    inv = ROPE_THETA ** (-jnp.arange(ROPE_AXIS_PAIRS, dtype=F32) / ROPE_AXIS_PAIRS)
    ang = jnp.concatenate([r[:, None] * inv, col[:, None] * inv], axis=-1)
    cos, sin = jnp.cos(ang), jnp.sin(ang)
    cos_f = jnp.concatenate([cos, cos], axis=-1)
    sin_f = jnp.concatenate([-sin, sin], axis=-1)
    cos_f = jnp.concatenate([jnp.ones((CTX_LEN, HEAD_DIM), F32), cos_f], axis=0)
    sin_f = jnp.concatenate([jnp.zeros((CTX_LEN, HEAD_DIM), F32), sin_f], axis=0)
    return cos_f, sin_f


def _qkv_kernel(x_ref, g_ref, m_ref, w_ref, qg_ref, kg_ref, cos_ref, sin_ref, q_ref, k_ref, v_ref):
    m = m_ref[0]
    h = _modulate(x_ref[...], g_ref[...], m[0:1], m[1:2]).astype(BF16)
    a = _dot(h, w_ref[...])
    cos, sin = cos_ref[...], sin_ref[...]
    nq = N_HEADS * HEAD_DIM
    nk = N_KV_HEADS * HEAD_DIM

    def norm_rope(u, gain):
        ms = jnp.mean(u * u, axis=-1, keepdims=True)
        un = u * lax.rsqrt(ms + NORM_EPS) * gain
        return un * cos + pltpu.roll(un, HEAD_DIM // 2, 1) * sin

    for hh in range(N_HEADS):
        sl = slice(hh * HEAD_DIM, (hh + 1) * HEAD_DIM)
        q_ref[:, sl] = (norm_rope(a[:, sl], qg_ref[...]) * (HEAD_DIM ** -0.5 * math.log2(math.e))).astype(q_ref.dtype)
    for hh in range(N_KV_HEADS):
        sl = slice(hh * HEAD_DIM, (hh + 1) * HEAD_DIM)
        k_ref[:, sl] = norm_rope(a[:, nq + hh * HEAD_DIM:nq + (hh + 1) * HEAD_DIM], kg_ref[...]).astype(k_ref.dtype)
    ones = jnp.ones((a.shape[0], HEAD_DIM), v_ref.dtype)
    for hh in range(N_KV_HEADS):
        v_ref[:, 2 * hh * HEAD_DIM:(2 * hh + 1) * HEAD_DIM] = (
            a[:, nq + nk + hh * HEAD_DIM:nq + nk + (hh + 1) * HEAD_DIM].astype(v_ref.dtype))
        v_ref[:, (2 * hh + 1) * HEAD_DIM:(2 * hh + 2) * HEAD_DIM] = ones


def _attn_kernel(q_ref, k_ref, v_ref, o_ref, s_ref, m_ref, acc_ref):
    i = pl.program_id(1)
    t = k_ref.shape[0]
    nt_dims = (((1,), (1,)), ((), ()))
    m_ref[...] = jnp.full(m_ref.shape, -jnp.inf, F32)
    acc_ref[...] = jnp.zeros(acc_ref.shape, F32)

    def scores(slot, start, size):
        kc = k_ref[pl.ds(start, size), :]
        for j in range(Q_PER_KV):
            s_ref[slot, j, :, 0:size] = lax.dot_general(q_ref[:, j * HEAD_DIM:(j + 1) * HEAD_DIM], kc, nt_dims,
                                                        preferred_element_type=F32)

    def update(slot, start, size):
        vc = v_ref[pl.ds(start, size), :]
        for j in range(Q_PER_KV):
            s = s_ref[slot, j, :, 0:size]
            m_old = m_ref[j]
            m_new = jnp.maximum(m_old, jnp.max(s, axis=-1, keepdims=True))
            alpha = jnp.exp2(m_old - m_new)
            p = jnp.exp2((s - jnp.tile(m_new, (1, size // LANES))).astype(BF16))
            acc_ref[j] = jnp.tile(alpha, (1, 2)) * acc_ref[j] + _dot(p, vc)
            m_ref[j] = m_new

    scores(0, 0, CTX_LEN)
    update(0, 0, CTX_LEN)

    @pl.when(i > 0)
    def _():
        n_pairs = (t - CTX_LEN) // (2 * ATT_TK)
        scores(0, CTX_LEN, ATT_TK)

        def body(c, carry):
            base = pl.multiple_of(CTX_LEN + 2 * c * ATT_TK, CTX_LEN)
            scores(1, base + ATT_TK, ATT_TK)
            update(0, base, ATT_TK)
            nxt = pl.multiple_of(jnp.minimum(base + 2 * ATT_TK, t - ATT_TK), CTX_LEN)
            scores(0, nxt, ATT_TK)
            update(1, base + ATT_TK, ATT_TK)
            return carry

        lax.fori_loop(0, n_pairs, body, 0)

    for j in range(Q_PER_KV):
        acc = acc_ref[j]
        o_ref[:, j * HEAD_DIM:(j + 1) * HEAD_DIM] = (acc[:, :HEAD_DIM] / acc[:, HEAD_DIM:]).astype(o_ref.dtype)


def _attn_layer(xa, mods, g1, wq, wk, wv, wo, q_g, k_g):
    t, d = xa.shape
    nq, nk = N_HEADS * HEAD_DIM, N_KV_HEADS * HEAD_DIM
    w = jnp.concatenate([wq, wk, wv], axis=1).astype(BF16)
    cos_f, sin_f = _rope_tables(t - CTX_LEN)
    q, k, v = pl.pallas_call(
        _qkv_kernel,
        grid=(t // TM,),
        in_specs=[pl.BlockSpec((TM, d), lambda i: (i, 0)),
                  pl.BlockSpec((1, d), lambda i: (0, 0)),
                  pl.BlockSpec((1, 6, d), lambda i: (_mod_row(i), 0, 0)),
                  pl.BlockSpec((d, nq + 2 * nk), lambda i: (0, 0)),
                  pl.BlockSpec((1, HEAD_DIM), lambda i: (0, 0)),
                  pl.BlockSpec((1, HEAD_DIM), lambda i: (0, 0)),
                  pl.BlockSpec((TM, HEAD_DIM), lambda i: (i, 0)),
                  pl.BlockSpec((TM, HEAD_DIM), lambda i: (i, 0))],
        out_specs=[pl.BlockSpec((TM, nq), lambda i: (i, 0)),
                   pl.BlockSpec((TM, nk), lambda i: (i, 0)),
                   pl.BlockSpec((TM, 2 * nk), lambda i: (i, 0))],
        out_shape=[jax.ShapeDtypeStruct((t, nq), BF16),
                   jax.ShapeDtypeStruct((t, nk), BF16),
                   jax.ShapeDtypeStruct((t, 2 * nk), BF16)],
        compiler_params=_params(("parallel",)),
        name="qkv",
    )(xa, g1.reshape(1, d), mods, w, q_g.reshape(1, HEAD_DIM), k_g.reshape(1, HEAD_DIM), cos_f, sin_f)
    qw = Q_PER_KV * HEAD_DIM
    o = pl.pallas_call(
        _attn_kernel,
        grid=(N_KV_HEADS, t // TM),
        in_specs=[pl.BlockSpec((TM, qw), lambda n, i: (i, n)),
                  pl.BlockSpec((t, HEAD_DIM), lambda n, i: (0, n)),
                  pl.BlockSpec((t, 2 * HEAD_DIM), lambda n, i: (0, n))],
        out_specs=pl.BlockSpec((TM, qw), lambda n, i: (i, n)),
        out_shape=jax.ShapeDtypeStruct((t, nq), BF16),
        scratch_shapes=[pltpu.VMEM((2, Q_PER_KV, TM, ATT_TK), F32),
                        pltpu.VMEM((Q_PER_KV, TM, LANES), F32),
                        pltpu.VMEM((Q_PER_KV, TM, 2 * HEAD_DIM), F32)],
        compiler_params=_params(("parallel", "parallel")),
        name="attention",
    )(q, k, v)
    return _mm_resid(o, wo.astype(BF16), jnp.zeros((d,), F32), xa, mods, 2)


def _s5_matrices(a_re, a_im, log_dt, b_re, b_im, c_re, c_im, reverse):
    l, g, p, c = SSM_SUB, SSM_GROUPS, SSM_STATE, SSM_GROUP_DIM
    a = lax.complex(a_re.astype(F32), a_im.astype(F32))
    adt = a * jnp.exp(log_dt.astype(F32))[:, None]
    a_bar = jnp.exp(adt)
    b_bar = ((a_bar - 1) / a)[..., None] * lax.complex(b_re.astype(F32), b_im.astype(F32))
    c_mat = lax.complex(c_re.astype(F32), c_im.astype(F32))
    pw = jnp.exp(adt[:, None, :] * jnp.arange(l + 1, dtype=F32)[None, :, None])
    kern = jnp.einsum('gcp,gkp,gpd->gkcd', c_mat, pw[:, :l], b_bar,
                      precision=lax.Precision.HIGHEST).real
    e = pw[:, l - 1 - jnp.arange(l), :][:, :, None, :] * jnp.transpose(b_bar, (0, 2, 1))[:, None, :, :]
    cp = jnp.transpose(c_mat[:, None, :, :] * pw[:, 1:, None, :], (0, 3, 1, 2))

    nb, gpb = SSM_CB, SSM_GPB
    ch_grp = jnp.arange(LANES) // c
    st_grp = jnp.arange(gpb * p) // p
    kfull = jnp.transpose(kern, (0, 1, 3, 2)).reshape(nb, gpb, l, c, c)
    kfull = jnp.transpose(kfull, (0, 2, 1, 3, 4)).reshape(nb, l, LANES, c)
    kfull = jnp.tile(kfull, (1, 1, 1, gpb)) * (ch_grp[:, None] == ch_grp[None, :])
    lag = jnp.arange(l)[None, :] - jnp.arange(l)[:, None]
    m = jnp.where((lag >= 0)[None, :, :, None, None], kfull[:, jnp.clip(lag, 0, l - 1)], 0.0)

    def widen_e(x):
        x = jnp.transpose(x.reshape(nb, gpb, l, c, p), (0, 2, 1, 3, 4)).reshape(nb, l, LANES, p)
        return jnp.tile(x, (1, 1, 1, gpb)) * (ch_grp[:, None] == st_grp[None, :])

    def widen_c(x):
        x = x.reshape(nb, gpb * p, l, c)
        return jnp.tile(x, (1, 1, 1, gpb)) * (st_grp[:, None, None] == ch_grp[None, None, :])

    e_re, e_im = widen_e(e.real), widen_e(e.imag)
    cp_re, cp_im = widen_c(cp.real), widen_c(-cp.imag)
    if reverse:
        m = jnp.flip(m, (1, 2))
        e_re, e_im = jnp.flip(e_re, 1), jnp.flip(e_im, 1)
        cp_re, cp_im = jnp.flip(cp_re, 2), jnp.flip(cp_im, 2)
    mbig = jnp.transpose(m.astype(BF16), (0, 1, 3, 2, 4)).reshape(nb, l * LANES, l * LANES)
    w1 = jnp.concatenate([e_re, e_im], axis=3).astype(BF16).reshape(nb, l * LANES, 2 * gpb * p)
    w2 = jnp.concatenate([mbig, cp_re.astype(BF16).reshape(nb, gpb * p, l * LANES),
                          cp_im.astype(BF16).reshape(nb, gpb * p, l * LANES)], axis=1)
    al = pw[:, l].reshape(SSM_CB, 1, SSM_GPB * p)
    return w1, w2, al.real, al.imag


def _s5_scan_kernel(*refs, reverse):
    l = SSM_SUB
    u_refs = refs[:l]
    w1_ref, w2_ref, ar_ref, ai_ref, h0_ref, y_ref, hfin_ref, v_ref, h_ref = refs[l:]
    ns = ar_ref.shape[-1]
    rows = v_ref.shape[0]

    @pl.when(pl.program_id(1) == 0)
    def _():
        h_ref[...] = h0_ref[0]

    u = jnp.concatenate([r[...] for r in u_refs], axis=1)
    v_ref[...] = _dot(u, w1_ref[0])
    ar, ai = ar_ref[0], ai_ref[0]

    def body(step, carry):
        hr, hi = carry
        j = rows - 1 - step if reverse else step
        vr = v_ref[pl.ds(j, 1), 0:ns]
        vi = v_ref[pl.ds(j, 1), ns:2 * ns]
        v_ref[pl.ds(j, 1), 0:ns] = hr
        v_ref[pl.ds(j, 1), ns:2 * ns] = hi
        return ar * hr - ai * hi + vr, ar * hi + ai * hr + vi

    hr, hi = lax.fori_loop(0, rows, body, (h_ref[0:1, :], h_ref[1:2, :]))
    h_ref[0:1, :] = hr
    h_ref[1:2, :] = hi
    hfin_ref[0] = h_ref[...]
    y = _dot(jnp.concatenate([u, v_ref[...].astype(BF16)], axis=1), w2_ref[0])
    for t in range(l):
        y_ref[t] = y[:, t * LANES:(t + 1) * LANES]


def _s5_scan(seq, mats, h0, reverse):
    t, d = seq.shape
    l = SSM_SUB
    w1, w2, al_re, al_im = mats
    n_sub = t // l
    rows = max(r for r in range(8, min(n_sub, 512) + 1, 8) if n_sub % r == 0)
    nr = n_sub // rows
    ns = SSM_GPB * SSM_STATE
    u2 = seq.reshape(n_sub, l * d)
    row_blk = (lambda r: nr - 1 - r) if reverse else (lambda r: r)
    blk = lambda s: pl.BlockSpec((rows, LANES), lambda cb, r, s=s: (row_blk(r), s * SSM_CB + cb))
    return pl.pallas_call(
        functools.partial(_s5_scan_kernel, reverse=reverse),
        grid=(SSM_CB, nr),
        in_specs=[blk(s) for s in range(l)] + [
            pl.BlockSpec((1,) + w1.shape[1:], lambda cb, r: (cb, 0, 0)),
            pl.BlockSpec((1,) + w2.shape[1:], lambda cb, r: (cb, 0, 0)),
            pl.BlockSpec((1, 1, ns), lambda cb, r: (cb, 0, 0)),
            pl.BlockSpec((1, 1, ns), lambda cb, r: (cb, 0, 0)),
            pl.BlockSpec((1, 2, ns), lambda cb, r: (cb, 0, 0))],
        out_specs=[pl.BlockSpec((l, rows, LANES), lambda cb, r: (0, row_blk(r), cb)),
                   pl.BlockSpec((1, 2, ns), lambda cb, r: (cb, 0, 0))],
        out_shape=[jax.ShapeDtypeStruct((l, n_sub, d), F32),
                   jax.ShapeDtypeStruct((SSM_CB, 2, ns), F32)],
        scratch_shapes=[pltpu.VMEM((rows, 2 * ns), F32), pltpu.VMEM((2, ns), F32)],
        compiler_params=_params(("parallel", "arbitrary")),
        name="s5_scan",
    )(*([u2] * l), w1, w2, al_re, al_im, h0)


def _gelu_tanh(x):
    return 0.5 * x * (1.0 + jnp.tanh(math.sqrt(2.0 / math.pi) * (x + 0.044715 * x * x * x)))


def _s5_glu_kernel(x_ref, g_ref, m_ref, d_ref, yf_ref, yb_ref, wa_ref, wg_ref, ba_ref, bg_ref, xo_ref, mo_ref, o_ref):
    m = m_ref[0]
    h = _modulate(x_ref[...], g_ref[...], m[0:1], m[1:2])
    z = _gelu_tanh(d_ref[...] * h + yf_ref[0] + yb_ref[0]).astype(BF16)
    a = _dot(z, wa_ref[...]) + ba_ref[...]
    g = _dot(z, wg_ref[...]) + bg_ref[...]
    o_ref[...] = xo_ref[...] + mo_ref[0][2:3] * (a * jax.nn.sigmoid(g))


def _s5_layer(xa, mods, g1, a_re, a_im, log_dt, b_re, b_im, c_re, c_im, d_skip, glu_w, glu_b):
    t, d = xa.shape
    l = SSM_SUB
    h = _prep(xa, g1, mods, 0)
    h_ctx, h_lat = h[:CTX_LEN], h[CTX_LEN:]
    x_lat = xa[CTX_LEN:]
    ns = SSM_GPB * SSM_STATE
    ys = []
    for direction in range(2):
        rev = direction == 1
        mats = _s5_matrices(a_re[direction], a_im[direction], log_dt[direction], b_re[direction], b_im[direction],
                            c_re[direction], c_im[direction], rev)
        _, state = _s5_scan(h_ctx, mats, jnp.zeros((SSM_CB, 2, ns), F32), rev)
        y, _ = _s5_scan(h_lat, mats, state, rev)
        ys.append(y)
    tn = 1024
    nt = d // tn
    n_sub = (t - CTX_LEN) // l
    assert n_sub % TM == 0
    glu_w = glu_w.astype(BF16)
    out = pl.pallas_call(
        _s5_glu_kernel,
        grid=(nt, n_sub // TM, l),
        in_specs=[pl.BlockSpec((TM, d), lambda j, i, s: (i, s)),
                  pl.BlockSpec((1, d), lambda j, i, s: (0, 0)),
                  pl.BlockSpec((1, 6, d), lambda j, i, s: (1, 0, 0)),
                  pl.BlockSpec((1, d), lambda j, i, s: (0, 0)),
                  pl.BlockSpec((1, TM, d), lambda j, i, s: (s, i, 0)),
                  pl.BlockSpec((1, TM, d), lambda j, i, s: (s, i, 0)),
                  pl.BlockSpec((d, tn), lambda j, i, s: (0, j)),
                  pl.BlockSpec((d, tn), lambda j, i, s: (0, nt + j)),
                  pl.BlockSpec((1, tn), lambda j, i, s: (0, j)),
                  pl.BlockSpec((1, tn), lambda j, i, s: (0, nt + j)),
                  pl.BlockSpec((TM, tn), lambda j, i, s: (i, s * nt + j)),
                  pl.BlockSpec((1, 6, tn), lambda j, i, s: (1, 0, j))],
        out_specs=pl.BlockSpec((TM, tn), lambda j, i, s: (i, s * nt + j)),
        out_shape=jax.ShapeDtypeStruct((n_sub, l * d), F32),
        compiler_params=_params(("parallel", "parallel", "parallel")),
        name="s5_glu",
    )(x_lat.reshape(n_sub, l * d), g1.reshape(1, d), mods, d_skip.reshape(1, d), ys[0], ys[1], glu_w, glu_w,
      glu_b.reshape(1, 2 * d), glu_b.reshape(1, 2 * d), x_lat.reshape(n_sub, l * d), mods)
    return out.reshape(t - CTX_LEN, d)


def _router_kernel(x_ref, g_ref, m_ref, wr_ref, br_ref, tri_ref, h_ref, idx_ref, gate_ref, rank_ref, cnt_ref, base_ref):
    i = pl.program_id(0)

    @pl.when(i == 0)
    def _():
        base_ref[...] = jnp.zeros(base_ref.shape, F32)

    m = m_ref[0]
    h = _modulate(x_ref[...], g_ref[...], m[3:4], m[4:5])
    h_ref[...] = h.astype(h_ref.dtype)
    logits = lax.dot_general(wr_ref[...], h, (((1,), (1,)), ((), ())), precision=lax.Precision.HIGHEST,
                             preferred_element_type=F32) + br_ref[...]
    e_iota = lax.broadcasted_iota(jnp.int32, logits.shape, 0).astype(F32)
    base = base_ref[...]
    tops, idxs, ranks = [], [], []
    for _ in range(TOP_K):
        top = jnp.max(logits, axis=0, keepdims=True)
        idx = jnp.min(jnp.where(logits == top, e_iota, float(N_EXPERTS)), axis=0, keepdims=True)
        sel = e_iota == idx
        logits = jnp.where(sel, -jnp.inf, logits)
        onehot = sel.astype(BF16)
        before = _dot(onehot, tri_ref[...])
        ranks.append(jnp.sum(jnp.where(sel, base + before, 0.0), axis=0, keepdims=True))
        base = base + jnp.sum(sel.astype(F32), axis=1, keepdims=True)
        tops.append(top)
        idxs.append(idx)
    base_ref[...] = base
    cnt_ref[...] = jnp.broadcast_to(base, cnt_ref.shape)
    ex = [jnp.exp(v - tops[0]) for v in tops]
    den = ex[0] + ex[1] + ex[2] + ex[3]
    idx_ref[...] = jnp.concatenate(idxs, axis=0).astype(jnp.int32)
    gate_ref[...] = jnp.concatenate([v / den for v in ex], axis=0)
    rank_ref[...] = jnp.concatenate(ranks, axis=0).astype(jnp.int32)


def _ffn_kernel(be_ref, x_ref, w1_hbm, b1_ref, w2_hbm, b2_ref, o_ref, w1s_ref, w2s_ref, w1b_ref, w2b_ref, sem, *, layer):
    i = pl.program_id(0)
    expert = be_ref[0, i]
    used = be_ref[1, i] > 0
    first = jnp.logical_or(i == 0, expert != be_ref[0, jnp.maximum(i - 1, 0)])

    def weight_copies(e):
        return (pltpu.make_async_copy(w1_hbm.at[layer, e], w1s_ref, sem.at[0]),
                pltpu.make_async_copy(w2_hbm.at[layer, e], w2s_ref, sem.at[1]))

    @pl.when(i == 0)
    def _():
        for cp in weight_copies(expert):
            cp.start()

    @pl.when(jnp.logical_and(first, used))
    def _():
        for cp in weight_copies(expert):
            cp.wait()

        def cast(src, dst):
            def body(r, carry):
                rows = pl.ds(pl.multiple_of(r * WCAST_ROWS, WCAST_ROWS), WCAST_ROWS)
                dst[rows, :] = src[rows, :].astype(BF16)
                return carry

            lax.fori_loop(0, src.shape[0] // WCAST_ROWS, body, 0)

        cast(w1s_ref, w1b_ref)
        cast(w2s_ref, w2b_ref)
        nxt = be_ref[2, i]

        @pl.when(nxt >= 0)
        def _():
            for cp in weight_copies(nxt):
                cp.start()

    @pl.when(jnp.logical_not(used))
    def _():
        o_ref[...] = jnp.zeros(o_ref.shape, o_ref.dtype)

    @pl.when(used)
    def _():
        h = _dot(x_ref[...], w1b_ref[...]) + b1_ref[0]
        gate = jnp.minimum(h[:, :D_EXPERT], SWIGLU_LIMIT)
        lin = jnp.clip(h[:, D_EXPERT:], -SWIGLU_LIMIT, SWIGLU_LIMIT)
        act = ((lin + 1.0) * gate * jax.nn.sigmoid(SWIGLU_ALPHA * gate)).astype(BF16)
        o_ref[...] = (_dot(act, w2b_ref[...]) + b2_ref[0]).astype(o_ref.dtype)


def _combine_kernel(y_ref, gt_ref, x_ref, m_ref, o_ref):
    gt = gt_ref[...]
    acc = gt[:, 0:1] * y_ref[0].astype(F32)
    for k in range(1, TOP_K):
        acc = acc + gt[:, k:k + 1] * y_ref[k].astype(F32)
    o_ref[...] = x_ref[...] + m_ref[0][5:6] * acc


def _moe_layer(xa, mods, g2, w_r, b_r, w1, b1, w2, b2, layer):
    t, d = xa.shape
    nb = t // TM
    tri = jnp.asarray(np.triu(np.ones((TM, TM), np.float32), 1), BF16)
    h2, idx, gates, rank, cnt = pl.pallas_call(
        _router_kernel,
        grid=(nb,),
        in_specs=[pl.BlockSpec((TM, d), lambda i: (i, 0)),
                  pl.BlockSpec((1, d), lambda i: (0, 0)),
                  pl.BlockSpec((1, 6, d), lambda i: (_mod_row(i), 0, 0)),
                  pl.BlockSpec((N_EXPERTS, d), lambda i: (0, 0)),
                  pl.BlockSpec((N_EXPERTS, 1), lambda i: (0, 0)),
                  pl.BlockSpec((TM, TM), lambda i: (0, 0))],
        out_specs=[pl.BlockSpec((TM, d), lambda i: (i, 0)),
                   pl.BlockSpec((TOP_K, TM), lambda i: (0, i)),
                   pl.BlockSpec((TOP_K, TM), lambda i: (0, i)),
                   pl.BlockSpec((TOP_K, TM), lambda i: (0, i)),
                   pl.BlockSpec((N_EXPERTS, LANES), lambda i: (0, 0))],
        out_shape=[jax.ShapeDtypeStruct((t, d), BF16),
                   jax.ShapeDtypeStruct((TOP_K, t), jnp.int32),
                   jax.ShapeDtypeStruct((TOP_K, t), F32),
                   jax.ShapeDtypeStruct((TOP_K, t), jnp.int32),
                   jax.ShapeDtypeStruct((N_EXPERTS, LANES), F32)],
        scratch_shapes=[pltpu.VMEM((N_EXPERTS, 1), F32)],
        compiler_params=_params(("arbitrary",)),
        name="router",
    )(xa, g2.reshape(1, d), mods, w_r.T, b_r.reshape(N_EXPERTS, 1), tri)

    counts = cnt[:, 0].astype(jnp.int32)
    padded = (counts + MOE_TM - 1) // MOE_TM * MOE_TM
    pend = jnp.cumsum(padded)
    pstart = pend - padded
    dest = pstart[idx] + rank
    n_rows = -(-t * TOP_K // MOE_TM) * MOE_TM + N_EXPERTS * MOE_TM
    n_blocks = n_rows // MOE_TM
    blk_start = jnp.arange(n_blocks, dtype=jnp.int32) * MOE_TM
    blk_used = (blk_start < pend[-1]).astype(jnp.int32)
    blk_e = jnp.minimum(jnp.sum((pend[None, :] <= blk_start[:, None]).astype(jnp.int32), axis=1), N_EXPERTS - 1)
    last_e = blk_e[jnp.maximum(pend[-1] // MOE_TM - 1, 0)]
    blk_e = jnp.where(blk_used > 0, blk_e, last_e)
    e_ids = jnp.arange(N_EXPERTS, dtype=jnp.int32)
    later_used = jnp.logical_and(e_ids[None, :] > e_ids[:, None], (counts > 0)[None, :])
    next_e = jnp.min(jnp.where(later_used, e_ids[None, :], N_EXPERTS), axis=1)
    next_e = jnp.where(next_e == N_EXPERTS, -1, next_e).astype(jnp.int32)
    be = jnp.stack([blk_e, blk_used, next_e[blk_e]])
    tok = jnp.broadcast_to(jnp.arange(t, dtype=jnp.int32)[None, :], (TOP_K, t))
    row_tok = (jnp.arange(n_rows, dtype=jnp.int32) % t).at[dest.reshape(-1)].set(
        tok.reshape(-1), unique_indices=True, mode='promise_in_bounds')
    xs = h2.at[row_tok].get(mode='promise_in_bounds')

    ys = pl.pallas_call(
        functools.partial(_ffn_kernel, layer=layer),
        grid_spec=pltpu.PrefetchScalarGridSpec(
            num_scalar_prefetch=1,
            grid=(n_blocks,),
            in_specs=[pl.BlockSpec((MOE_TM, d), lambda i, be: (i, 0)),
                      pl.BlockSpec(memory_space=pl.ANY),
                      pl.BlockSpec((1, 1, 2 * D_EXPERT), lambda i, be: (be[0, i], 0, 0)),
                      pl.BlockSpec(memory_space=pl.ANY),
                      pl.BlockSpec((1, 1, d), lambda i, be: (be[0, i], 0, 0))],
            out_specs=pl.BlockSpec((MOE_TM, d), lambda i, be: (i, 0)),
            scratch_shapes=[pltpu.VMEM((d, 2 * D_EXPERT), F32), pltpu.VMEM((D_EXPERT, d), F32),
                            pltpu.VMEM((d, 2 * D_EXPERT), BF16), pltpu.VMEM((D_EXPERT, d), BF16),
                            pltpu.SemaphoreType.DMA((2,))]),
        out_shape=jax.ShapeDtypeStruct((n_rows, d), BF16),
        compiler_params=_params(("arbitrary",)),
        name="moe_ffn",
    )(be, xs, w1, b1.reshape(N_EXPERTS, 1, 2 * D_EXPERT), w2, b2.reshape(N_EXPERTS, 1, d))

    y4 = ys.at[dest.reshape(-1)].get(mode='promise_in_bounds', unique_indices=True).reshape(TOP_K, t, d)
    return pl.pallas_call(
        _combine_kernel,
        grid=(nb,),
        in_specs=[pl.BlockSpec((TOP_K, TM, d), lambda i: (0, i, 0)),
                  pl.BlockSpec((TM, TOP_K), lambda i: (i, 0)),
                  pl.BlockSpec((TM, d), lambda i: (i, 0)),
                  pl.BlockSpec((1, 6, d), lambda i: (_mod_row(i), 0, 0))],
        out_specs=pl.BlockSpec((TM, d), lambda i: (i, 0)),
        out_shape=jax.ShapeDtypeStruct((t, d), F32),
        compiler_params=_params(("parallel",)),
        name="moe_combine",
    )(y4, gates.T, xa, mods)


def kernel(x, c, ctx, c_ctx, ada_w, ada_b, norm1_g, norm2_g, fourier_w_out, fourier_b_out, conv_w_in, conv_k, conv_b, conv_w_out, attn_wq, attn_wk, attn_wv, attn_wo, attn_q_g, attn_k_g, ssm_a_re, ssm_a_im, ssm_log_dt, ssm_b_re, ssm_b_im, ssm_c_re, ssm_c_im, ssm_d, ssm_glu_w, ssm_glu_b, router_w, router_b, moe_w1, moe_b1, moe_w2, moe_b2):
    bsz, n, d = x.shape
    assert bsz == 1 and d == D_MODEL and ctx.shape[1] == CTX_LEN and n % TM == 0
    depth = ada_w.shape[0]
    cond8 = jnp.zeros((8, d), F32).at[0].set(c_ctx).at[1].set(c[0])
    mods = _ada_all(cond8, ada_w, ada_b).reshape(depth, 8, 6, d)
    assert depth == 4
    xa = jnp.concatenate([ctx[0], x[0]], axis=0)
    for i in range(depth):
        m, j = i % 4, i // 4
        mods_i = mods[i]
        if m == 0:
            xa = _fourier_layer(xa, mods[i], norm1_g[i], fourier_w_out[j], fourier_b_out[j])
        elif m == 1:
            xa = _conv_layer(xa, mods[i], norm1_g[i], conv_w_in[j], conv_k[j], conv_b[j], conv_w_out[j])
        elif m == 2:
            xa = _attn_layer(xa, mods[i], norm1_g[i], attn_wq[j], attn_wk[j], attn_wv[j], attn_wo[j],
                             attn_q_g[j], attn_k_g[j])
        else:
            xa = _s5_layer(xa, mods[i], norm1_g[i], ssm_a_re[j], ssm_a_im[j], ssm_log_dt[j], ssm_b_re[j],
                           ssm_b_im[j], ssm_c_re[j], ssm_c_im[j], ssm_d[j], ssm_glu_w[j], ssm_glu_b[j])
            mods_i = mods_i.at[0].set(mods_i[1])
        xa = _moe_layer(xa, mods_i, norm2_g[i], router_w[i], router_b[i], moe_w1, moe_b1[i], moe_w2, moe_b2[i], i)
    return xa[None]
```
